```python
import jax, jax.numpy as jnp
from jax import lax
import numpy as np

D_MODEL = 4096
BATCH = 1
SEQ = 16384
DEPTH = 4

CHUNK = 64
Q_BLOCK = 128
N_MIXERS = 2
MLA_HEADS = 8
MLA_Q_LORA = 1024
MLA_KV_LORA = 512
MLA_NOPE = 128
MLA_ROPE = 64
MLA_QK_DIM = MLA_NOPE + MLA_ROPE
MLA_V = 128
ROPE_THETA = 10000.0
SB_HEADS = 4
SB_HEAD_DIM = 256
D_FF = 4 * D_MODEL
PLE_DIM = 256
EPS = 1e-6

kernel_name = "hybrid_mla_stickbreaking_trunk"


def rmsnorm(x, gain):
    xf = x.astype(jnp.float32)
    y = xf * lax.rsqrt(jnp.mean(xf * xf, axis=-1, keepdims=True) + EPS)
    return (y * gain.astype(jnp.float32)).astype(x.dtype)


def rope_tables(positions, dtype):
    inv_freq = ROPE_THETA ** (-jnp.arange(0, MLA_ROPE, 2, dtype=jnp.float32) / MLA_ROPE)
    ang = positions.astype(jnp.float32)[..., None] * inv_freq
    return jnp.cos(ang)[:, :, None, :].astype(dtype), jnp.sin(ang)[:, :, None, :].astype(dtype)


def rope_tail(x, cos, sin):
    x_nope, x_rope = x[..., :MLA_NOPE], x[..., MLA_NOPE:]
    x1, x2 = jnp.split(x_rope, 2, axis=-1)
    rot = jnp.concatenate([x1 * cos - x2 * sin, x2 * cos + x1 * sin], axis=-1)
    return jnp.concatenate([x_nope, rot], axis=-1)


def chunk_causal_softmax_attention(q, k, v):
    S = q.shape[1]
    outs = []
    for b in range(S // Q_BLOCK):
        end = (b + 1) * Q_BLOCK
        t = b * Q_BLOCK + jnp.arange(Q_BLOCK)
        kpos = jnp.arange(end)
        mask = (kpos[None, :] // CHUNK) <= (t[:, None] // CHUNK)
        s = jnp.einsum('bqhd,bkhd->bhqk', q[:, b * Q_BLOCK:end], k[:, :end],
                       preferred_element_type=jnp.float32)
        s = jnp.where(mask[None, None], s, -jnp.inf)
        e = jnp.exp(s - jnp.max(s, axis=-1, keepdims=True))
        denom = jnp.sum(e, axis=-1)
        o = jnp.einsum('bhqk,bkhd->bqhd', e.astype(v.dtype), v[:, :end],
                       preferred_element_type=jnp.float32)
        o = o / jnp.swapaxes(denom, 1, 2)[..., None]
        outs.append(o.astype(v.dtype))
    return jnp.concatenate(outs, axis=1)


def stick_breaking_attention(q, k, v):
    S = q.shape[1]
    B, _, H, _ = q.shape
    later_in_block = (jnp.arange(Q_BLOCK)[:, None] > jnp.arange(Q_BLOCK)[None, :]).astype(jnp.float32)
    outs = []
    for b in range(S // Q_BLOCK):
        end = (b + 1) * Q_BLOCK
        nk = b + 1
        t = b * Q_BLOCK + jnp.arange(Q_BLOCK)
        mask = jnp.arange(end)[None, :] < t[:, None]
        z = jnp.einsum('bqhd,bkhd->bhqk', q[:, b * Q_BLOCK:end], k[:, :end],
                       preferred_element_type=jnp.float32)
        z = jnp.where(mask[None, None], z, -jnp.inf)
        sp = jnp.maximum(z, 0.0) + jnp.log1p(jnp.exp(-jnp.abs(z)))
        spb = sp.reshape(B, H, Q_BLOCK, nk, Q_BLOCK)
        within = jnp.einsum('bhqnj,js->bhqns', spb, later_in_block)
        tot = jnp.sum(spb, axis=-1)
        after = lax.cumsum(tot, axis=3, reverse=True) - tot
        later = (within + after[..., None]).reshape(B, H, Q_BLOCK, end)
        a = jnp.exp(z - (sp + later))
        o = jnp.einsum('bhqk,bkhd->bqhd', a.astype(v.dtype), v[:, :end],
                       preferred_element_type=jnp.float32)
        outs.append(o.astype(v.dtype))
    return jnp.concatenate(outs, axis=1)


def mla_mixer(h, w_in, q_norm, kv_norm, w_uq, w_ukv, q_gain, k_gain, w_o, cos, sin):
    B, S, _ = h.shape
    proj = h @ w_in
    c_q = rmsnorm(proj[..., :MLA_Q_LORA], q_norm)
    c_kv = rmsnorm(proj[..., MLA_Q_LORA:MLA_Q_LORA + MLA_KV_LORA], kv_norm)
    k_rope = proj[..., MLA_Q_LORA + MLA_KV_LORA:]
    q = (c_q @ w_uq).reshape(B, S, MLA_HEADS, MLA_QK_DIM)
    kv = (c_kv @ w_ukv).reshape(B, S, MLA_HEADS, MLA_NOPE + MLA_V)
    k_nope, v = kv[..., :MLA_NOPE], kv[..., MLA_NOPE:]
    k = jnp.concatenate(
        [k_nope, jnp.broadcast_to(k_rope[:, :, None, :], (B, S, MLA_HEADS, MLA_ROPE))], axis=-1)
    q = rope_tail(rmsnorm(q, q_gain), cos, sin) * (MLA_QK_DIM ** -0.5)
    k = rope_tail(rmsnorm(k, k_gain), cos, sin)
    o = chunk_causal_softmax_attention(q, k, v)
    return o.reshape(B, S, MLA_HEADS * MLA_V) @ w_o


def sb_mixer(h, w_qkv, w_o):
    B, S, _ = h.shape
    qkv = (h @ w_qkv).reshape(B, S, 3, SB_HEADS, SB_HEAD_DIM)
    q, k, v = qkv[:, :, 0] * (SB_HEAD_DIM ** -0.5), qkv[:, :, 1], qkv[:, :, 2]
    o = stick_breaking_attention(q, k, v)
    return o.reshape(B, S, SB_HEADS * SB_HEAD_DIM) @ w_o


def squared_relu_mlp(h, w_up, w_down):
    return jnp.square(jax.nn.relu(h @ w_up)) @ w_down


def setup_inputs(seed: int = 0) -> dict:
    key = jax.random.key(seed)
    ks = jax.random.split(key, 24)
    n_a = (DEPTH + 1) // 2
    n_b = DEPTH // 2

    def w(k, shape, fan_in):
        return jax.random.normal(k, shape, jnp.float32) * (fan_in ** -0.5)

    def gain(k, shape):
        return 1.0 + 0.02 * jax.random.normal(k, shape, jnp.float32)

    return {
        "x": jax.random.normal(ks[0], (BATCH, SEQ, D_MODEL), jnp.float32),
        "p": jax.random.normal(ks[1], (DEPTH, BATCH, SEQ, PLE_DIM), jnp.float32),
        "positions": jnp.broadcast_to(jnp.arange(SEQ, dtype=jnp.int32), (BATCH, SEQ)),
        "norm_mix": gain(ks[2], (DEPTH, D_MODEL)),
        "norm_mlp": gain(ks[3], (DEPTH, D_MODEL)),
        "norm_ple": gain(ks[4], (DEPTH, D_MODEL)),
        "mla_w_in": w(ks[5], (n_a, D_MODEL, MLA_Q_LORA + MLA_KV_LORA + MLA_ROPE), D_MODEL),
        "mla_q_norm": gain(ks[6], (n_a, MLA_Q_LORA)),
        "mla_kv_norm": gain(ks[7], (n_a, MLA_KV_LORA)),
        "mla_w_uq": w(ks[8], (n_a, MLA_Q_LORA, MLA_HEADS * MLA_QK_DIM), MLA_Q_LORA),
        "mla_w_ukv": w(ks[9], (n_a, MLA_KV_LORA, MLA_HEADS * (MLA_NOPE + MLA_V)), MLA_KV_LORA),
        "mla_q_gain": gain(ks[10], (n_a, MLA_QK_DIM)),
        "mla_k_gain": gain(ks[11], (n_a, MLA_QK_DIM)),
        "mla_w_o": w(ks[12], (n_a, MLA_HEADS * MLA_V, D_MODEL), MLA_HEADS * MLA_V),
        "sb_w_qkv": w(ks[13], (n_b, D_MODEL, 3 * SB_HEADS * SB_HEAD_DIM), D_MODEL),
        "sb_w_o": w(ks[14], (n_b, SB_HEADS * SB_HEAD_DIM, D_MODEL), SB_HEADS * SB_HEAD_DIM),
        "mlp_w_up": w(ks[15], (DEPTH, D_MODEL, D_FF), D_MODEL),
        "mlp_w_down": w(ks[16], (DEPTH, D_FF, D_MODEL), D_FF),
        "ple_w_proj": w(ks[17], (DEPTH, PLE_DIM, D_MODEL), PLE_DIM),
        "ple_w_gate": w(ks[18], (DEPTH, D_MODEL, D_MODEL), D_MODEL),
    }


def reference(x, p, positions, norm_mix, norm_mlp, norm_ple,
              mla_w_in, mla_q_norm, mla_kv_norm, mla_w_uq, mla_w_ukv,
              mla_q_gain, mla_k_gain, mla_w_o,
              sb_w_qkv, sb_w_o, mlp_w_up, mlp_w_down, ple_w_proj, ple_w_gate):
    cos, sin = rope_tables(positions, x.dtype)
    h = x
    for i in range(DEPTH):
        hn = rmsnorm(h, norm_mix[i])
        j = i // N_MIXERS
        if i % N_MIXERS == 0:
            mix = mla_mixer(hn, mla_w_in[j], mla_q_norm[j], mla_kv_norm[j], mla_w_uq[j],
                            mla_w_ukv[j], mla_q_gain[j], mla_k_gain[j], mla_w_o[j], cos, sin)
        else:
            mix = sb_mixer(hn, sb_w_qkv[j], sb_w_o[j])
        h = h + mix
        h = h + squared_relu_mlp(rmsnorm(h, norm_mlp[i]), mlp_w_up[i], mlp_w_down[i])
        gate = jax.nn.sigmoid(rmsnorm(h, norm_ple[i]) @ ple_w_gate[i])
        h = h + (p[i] @ ple_w_proj[i]) * gate
    return h
```

```python
import functools

import jax
import jax.numpy as jnp
from jax import lax
from jax.experimental import pallas as pl
from jax.experimental.pallas import tpu as pltpu

F32 = jnp.float32
BF16 = jnp.bfloat16

CHUNK = 64
N_MIXERS = 2
MLA_HEADS = 8
MLA_Q_LORA = 1024
MLA_KV_LORA = 512
MLA_NOPE = 128
MLA_ROPE = 64
MLA_QK = MLA_NOPE + MLA_ROPE
MLA_QK_PAD = 256
MLA_V = 128
ROPE_THETA = 10000.0
SB_HEADS = 4
SB_HEAD_DIM = 256
EPS = 1e-6

MIB = 1024 * 1024
V7X_VMEM_BYTES = 64 * MIB


def _params(semantics, vmem_mib):
    assert vmem_mib * MIB < V7X_VMEM_BYTES
    return pltpu.CompilerParams(dimension_semantics=semantics, vmem_limit_bytes=vmem_mib * MIB)


def _resident(shape):
    return pl.BlockSpec(shape, lambda *_: (0,) * len(shape), pipeline_mode=pl.Buffered(1))


def _rms_rows(x, gain):
    return x * lax.rsqrt(jnp.mean(x * x, axis=-1, keepdims=True) + EPS) * gain


def _rms_cols(x, gain):
    return x * lax.rsqrt(jnp.mean(x * x, axis=0, keepdims=True) + EPS) * gain


def _mla_proj_kernel(h_ref, g_ref, w_in_ref, qn_ref, kvn_ref, w_uq_ref, w_ukv_ref,
                     qg_ref, kg_ref, cos_ref, sin_ref, qT_ref, k_ref, vT_ref):
    tm = h_ref.shape[0]
    y = _rms_rows(h_ref[...], g_ref[...]).astype(BF16)
    projT = lax.dot_general(w_in_ref[...], y, (((1,), (1,)), ((), ())),
                            preferred_element_type=F32)
    cq = _rms_cols(projT[:MLA_Q_LORA], qn_ref[...]).astype(BF16)
    ckv = _rms_cols(projT[MLA_Q_LORA:MLA_Q_LORA + MLA_KV_LORA], kvn_ref[...]).astype(BF16)
    kr = projT[MLA_Q_LORA + MLA_KV_LORA:]
    qT = jnp.dot(w_uq_ref[...], cq, preferred_element_type=F32)
    kvT = jnp.dot(w_ukv_ref[...], ckv, preferred_element_type=F32)
    cos = cos_ref[...]
    sin = sin_ref[...]
    qg = qg_ref[...]
    kg = kg_ref[...]
    kr_ss = jnp.sum(kr * kr, axis=0, keepdims=True)
    scale = MLA_QK ** -0.5
    half = MLA_ROPE // 2
    pad = MLA_QK_PAD - MLA_QK
    for hd in range(MLA_HEADS):
        qh = _rms_cols(qT[hd * MLA_QK:(hd + 1) * MLA_QK], qg) * scale
        x1 = qh[MLA_NOPE:MLA_NOPE + half]
        x2 = qh[MLA_NOPE + half:]
        qT_ref[hd, 0:MLA_NOPE, :] = qh[:MLA_NOPE].astype(BF16)
        qT_ref[hd, MLA_NOPE:MLA_NOPE + half, :] = (x1 * cos - x2 * sin).astype(BF16)
        qT_ref[hd, MLA_NOPE + half:MLA_QK, :] = (x2 * cos + x1 * sin).astype(BF16)
        qT_ref[hd, MLA_QK:, :] = jnp.zeros((pad, tm), BF16)

        base = hd * (MLA_NOPE + MLA_V)
        kn = kvT[base:base + MLA_NOPE]
        inv = lax.rsqrt((jnp.sum(kn * kn, axis=0, keepdims=True) + kr_ss) * (1.0 / MLA_QK) + EPS)
        kn = kn * inv * kg[:MLA_NOPE]
        krh = kr * inv * kg[MLA_NOPE:]
        k1 = krh[:half]
        k2 = krh[half:]
        kT = jnp.concatenate(
            [kn, k1 * cos - k2 * sin, k2 * cos + k1 * sin, jnp.zeros((pad, tm), F32)], axis=0)
        k_ref[hd] = kT.T.astype(BF16)
        vT_ref[hd] = kvT[base + MLA_NOPE:base + MLA_NOPE + MLA_V].astype(BF16)


def _mla_proj(h, g, w_inT, qn, kvn, w_uqT, w_ukvT, qg, kg, cosT, sinT, tm=256):
    S, D = h.shape
    tm = min(tm, S)
    H = MLA_HEADS
    return pl.pallas_call(
        _mla_proj_kernel,
        grid=(S // tm,),
        in_specs=[
            pl.BlockSpec((tm, D), lambda i: (i, 0)),
            _resident((1, D)),
            _resident(w_inT.shape),
            _resident(qn.shape),
            _resident(kvn.shape),
            _resident(w_uqT.shape),
            _resident(w_ukvT.shape),
            _resident(qg.shape),
            _resident(kg.shape),
            pl.BlockSpec((MLA_ROPE // 2, tm), lambda i: (0, i)),
            pl.BlockSpec((MLA_ROPE // 2, tm), lambda i: (0, i)),
        ],
        out_specs=[
            pl.BlockSpec((H, MLA_QK_PAD, tm), lambda i: (0, 0, i)),
            pl.BlockSpec((H, tm, MLA_QK_PAD), lambda i: (0, i, 0)),
            pl.BlockSpec((H, MLA_V, tm), lambda i: (0, 0, i)),
        ],
        out_shape=[
            jax.ShapeDtypeStruct((H, MLA_QK_PAD, S), BF16),
            jax.ShapeDtypeStruct((H, S, MLA_QK_PAD), BF16),
            jax.ShapeDtypeStruct((H, MLA_V, S), BF16),
        ],
        compiler_params=_params(("parallel",), 52),
        name="mla_proj",
    )(h, g, w_inT, qn, kvn, w_uqT, w_ukvT, qg, kg, cosT, sinT)


def _mla_attn_kernel(qT_ref, k_ref, vT_ref, o_ref, m_ref, l_ref, acc_ref, *, T):
    i = pl.program_id(1)
    q = qT_ref[0]
    m_ref[...] = jnp.full(m_ref.shape, -jnp.inf, F32)
    l_ref[...] = jnp.zeros(l_ref.shape, F32)
    acc_ref[...] = jnp.zeros(acc_ref.shape, F32)

    def block(n, diagonal):
        start = pl.multiple_of(n * T, T)
        s = jnp.dot(k_ref[0, pl.ds(start, T), :], q, preferred_element_type=F32)
        if diagonal:
            kc = lax.broadcasted_iota(jnp.int32, (T, T), 0) // CHUNK
            qc = lax.broadcasted_iota(jnp.int32, (T, T), 1) // CHUNK
            s = jnp.where(kc <= qc, s, -jnp.inf)
        m_prev = m_ref[...]
        m_new = jnp.maximum(m_prev, jnp.max(s, axis=0, keepdims=True))
        alpha = jnp.exp(m_prev - m_new)
        p = jnp.exp(s - m_new)
        l_ref[...] = alpha * l_ref[...] + jnp.sum(p, axis=0, keepdims=True)
        pv = jnp.dot(vT_ref[0, :, pl.ds(start, T)], p.astype(BF16), preferred_element_type=F32)
        acc_ref[...] = alpha * acc_ref[...] + pv
        m_ref[...] = m_new

    def body(n, carry):
        block(n, False)
        return carry

    lax.fori_loop(0, i, body, 0)
    block(i, True)
    o_ref[...] = (acc_ref[...] / l_ref[...]).T.astype(BF16)


def _mla_attn(qT, k, vT, T=512):
    H, _, S = qT.shape
    T = min(T, S)
    return pl.pallas_call(
        functools.partial(_mla_attn_kernel, T=T),
        grid=(H, S // T),
        in_specs=[
            pl.BlockSpec((1, MLA_QK_PAD, T), lambda h, i: (h, 0, i)),
            pl.BlockSpec((1, S, MLA_QK_PAD), lambda h, i: (h, 0, 0)),
            pl.BlockSpec((1, MLA_V, S), lambda h, i: (h, 0, 0)),
        ],
        out_specs=pl.BlockSpec((T, MLA_V), lambda h, i: (i, h)),
        out_shape=jax.ShapeDtypeStruct((S, H * MLA_V), BF16),
        scratch_shapes=[
            pltpu.VMEM((1, T), F32),
            pltpu.VMEM((1, T), F32),
            pltpu.VMEM((MLA_V, T), F32),
        ],
        compiler_params=_params(("parallel", "arbitrary"), 48),
        name="mla_attn",
    )(qT, k, vT)


def _sb_proj_kernel(h_ref, g_ref, wq_ref, wkT_ref, wv_ref, q_ref, kT_ref, v_ref):
    y = _rms_rows(h_ref[...], g_ref[...]).astype(BF16)
    q = jnp.dot(y, wq_ref[...], preferred_element_type=F32)
    q_ref[...] = (q * (SB_HEAD_DIM ** -0.5)).astype(BF16)
    v_ref[...] = jnp.dot(y, wv_ref[...], preferred_element_type=F32).astype(BF16)
    kT = lax.dot_general(wkT_ref[...], y, (((1,), (1,)), ((), ())),
                         preferred_element_type=F32)
    for hd in range(SB_HEADS):
        kT_ref[hd] = kT[hd * SB_HEAD_DIM:(hd + 1) * SB_HEAD_DIM].astype(BF16)


def _sb_proj(h, g, wq, wkT, wv, tm=256):
    S, D = h.shape
    tm = min(tm, S)
    HD = SB_HEADS * SB_HEAD_DIM
    return pl.pallas_call(
        _sb_proj_kernel,
        grid=(S // tm,),
        in_specs=[
            pl.BlockSpec((tm, D), lambda i: (i, 0)),
            _resident((1, D)),
            _resident(wq.shape),
            _resident(wkT.shape),
            _resident(wv.shape),
        ],
        out_specs=[
            pl.BlockSpec((tm, HD), lambda i: (i, 0)),
            pl.BlockSpec((SB_HEADS, SB_HEAD_DIM, tm), lambda i: (0, 0, i)),
            pl.BlockSpec((tm, HD), lambda i: (i, 0)),
        ],
        out_shape=[
            jax.ShapeDtypeStruct((S, HD), BF16),
            jax.ShapeDtypeStruct((SB_HEADS, SB_HEAD_DIM, S), BF16),
            jax.ShapeDtypeStruct((S, HD), BF16),
        ],
        compiler_params=_params(("parallel",), 52),
        name="sb_proj",
    )(h, g, wq, wkT, wv)


def _sb_attn_kernel(q_ref, kT_ref, v_ref, o_ref, acc_ref, after_ref, *, T):
    i = pl.program_id(1)
    q = q_ref[...]
    row = lax.broadcasted_iota(jnp.int32, (T, T), 0)
    col = lax.broadcasted_iota(jnp.int32, (T, T), 1)
    later = (row > col).astype(BF16)
    acc_ref[...] = jnp.zeros(acc_ref.shape, F32)
    after_ref[...] = jnp.zeros(after_ref.shape, F32)

    def block(n, diagonal):
        start = pl.multiple_of(n * T, T)
        z = jnp.dot(q, kT_ref[0, :, pl.ds(start, T)], preferred_element_type=F32)
        sp = jnp.maximum(z, 0.0) + jnp.log(1.0 + jnp.exp(-jnp.abs(z)))
        if diagonal:
            valid = col < row
            sp = jnp.where(valid, sp, 0.0)
        within = jnp.dot(sp.astype(BF16), later, preferred_element_type=F32)
        after = after_ref[...]
        a = jnp.exp(z - sp - within - after)
        if diagonal:
            a = jnp.where(valid, a, 0.0)
        acc_ref[...] += jnp.dot(a.astype(BF16), v_ref[pl.ds(start, T), :],
                                preferred_element_type=F32)
        after_ref[...] = after + jnp.sum(sp, axis=1, keepdims=True)

    block(i, True)

    def body(j, carry):
        block(i - 1 - j, False)
        return carry

    lax.fori_loop(0, i, body, 0)
    o_ref[...] = acc_ref[...].astype(BF16)


def _sb_attn(q, kT, v, T=256):
    S = q.shape[0]
    T = min(T, S)
    Dh = SB_HEAD_DIM
    return pl.pallas_call(
        functools.partial(_sb_attn_kernel, T=T),
        grid=(SB_HEADS, S // T),
        in_specs=[
            pl.BlockSpec((T, Dh), lambda h, i: (i, h)),
            pl.BlockSpec((1, Dh, S), lambda h, i: (h, 0, 0)),
            pl.BlockSpec((S, Dh), lambda h, i: (0, h)),
        ],
        out_specs=pl.BlockSpec((T, Dh), lambda h, i: (i, h)),
        out_shape=jax.ShapeDtypeStruct((S, SB_HEADS * Dh), BF16),
        scratch_shapes=[
            pltpu.VMEM((T, Dh), F32),
            pltpu.VMEM((T, 1), F32),
        ],
        compiler_params=_params(("parallel", "arbitrary"), 48),
        name="sb_attn",
    )(q, kT, v)


def _out_proj_kernel(o_ref, h_ref, w_ref, g_ref, h1_ref, xn_ref):
    h1 = h_ref[...] + jnp.dot(o_ref[...], w_ref[...], preferred_element_type=F32)
    h1_ref[...] = h1
    xn_ref[...] = _rms_rows(h1, g_ref[...]).astype(BF16)


def _out_proj(o, h, w, g, tm=256):
    S, D = h.shape
    tm = min(tm, S)
    return pl.pallas_call(
        _out_proj_kernel,
        grid=(S // tm,),
        in_specs=[
            pl.BlockSpec((tm, o.shape[1]), lambda i: (i, 0)),
            pl.BlockSpec((tm, D), lambda i: (i, 0)),
            _resident(w.shape),
            _resident((1, D)),
        ],
        out_specs=[
            pl.BlockSpec((tm, D), lambda i: (i, 0)),
            pl.BlockSpec((tm, D), lambda i: (i, 0)),
        ],
        out_shape=[
            jax.ShapeDtypeStruct((S, D), F32),
            jax.ShapeDtypeStruct((S, D), BF16),
        ],
        compiler_params=_params(("parallel",), 48),
        name="out_proj",
    )(o, h, w, g)


def _mlp_up_kernel(x_ref, w_ref, u_ref):
    u = jnp.maximum(jnp.dot(x_ref[...], w_ref[...], preferred_element_type=F32), 0.0)
    u_ref[...] = (u * u).astype(BF16)


def _mlp_up(xn, w, tm=1024, tn=1024):
    S, D = xn.shape
    F = w.shape[1]
    tm, tn = min(tm, S), min(tn, F)
    return pl.pallas_call(
        _mlp_up_kernel,
        grid=(S // tm, F // tn),
        in_specs=[
            pl.BlockSpec((tm, D), lambda i, j: (i, 0)),
            pl.BlockSpec((D, tn), lambda i, j: (0, j)),
        ],
        out_specs=pl.BlockSpec((tm, tn), lambda i, j: (i, j)),
        out_shape=jax.ShapeDtypeStruct((S, F), BF16),
        compiler_params=_params(("parallel", "arbitrary"), 52),
        name="mlp_up",
    )(xn, w)


def _mlp_down_kernel(u_ref, w_ref, r_ref, o_ref):
    k = pl.program_id(2)
    d = jnp.dot(u_ref[...], w_ref[...], preferred_element_type=F32)

    @pl.when(k == 0)
    def _():
        o_ref[...] = r_ref[...] + d

    @pl.when(k > 0)
    def _():
        o_ref[...] += d


def _mlp_down(u, w, res, tm=1024, tn=2048, tk=1024):
    S, F = u.shape
    D = w.shape[1]
    tm, tn, tk = min(tm, S), min(tn, D), min(tk, F)
    return pl.pallas_call(
        _mlp_down_kernel,
        grid=(S // tm, D // tn, F // tk),
        in_specs=[
            pl.BlockSpec((tm, tk), lambda i, j, k: (i, k)),
            pl.BlockSpec((tk, tn), lambda i, j, k: (k, j)),
            pl.BlockSpec((tm, tn), lambda i, j, k: (i, j)),
        ],
        out_specs=pl.BlockSpec((tm, tn), lambda i, j, k: (i, j)),
        out_shape=jax.ShapeDtypeStruct((S, D), F32),
        compiler_params=_params(("parallel", "parallel", "arbitrary"), 52),
        name="mlp_down",
    )(u, w, res)


def _ple_kernel(hrow_ref, hcol_ref, g_ref, wg_ref, p_ref, wp_ref, o_ref, xn_ref):
    @pl.when(pl.program_id(1) == 0)
    def _():
        xn_ref[...] = _rms_rows(hrow_ref[...], g_ref[...]).astype(BF16)

    logits = jnp.dot(xn_ref[...], wg_ref[...], preferred_element_type=F32)
    gate = 1.0 / (1.0 + jnp.exp(-logits))
    emb = jnp.dot(p_ref[...].astype(BF16), wp_ref[...], preferred_element_type=F32)
    o_ref[...] = hcol_ref[...] + emb * gate


def _ple(h, g, wg, p, wp, tm=512, tn=512):
    S, D = h.shape
    P = p.shape[1]
    tm, tn = min(tm, S), min(tn, D)
    return pl.pallas_call(
        _ple_kernel,
        grid=(S // tm, D // tn),
        in_specs=[
            pl.BlockSpec((tm, D), lambda i, j: (i, 0)),
            pl.BlockSpec((tm, tn), lambda i, j: (i, j)),
            _resident((1, D)),
            pl.BlockSpec((D, tn), lambda i, j: (0, j)),
            pl.BlockSpec((tm, P), lambda i, j: (i, 0)),
            pl.BlockSpec((P, tn), lambda i, j: (0, j)),
        ],
        out_specs=pl.BlockSpec((tm, tn), lambda i, j: (i, j)),
        out_shape=jax.ShapeDtypeStruct((S, D), F32),
        scratch_shapes=[pltpu.VMEM((tm, D), BF16)],
        compiler_params=_params(("parallel", "arbitrary"), 52),
        name="ple_gate",
    )(h, h, g, wg, p, wp)


def _rope_tables(positions):
    inv_freq = ROPE_THETA ** (-jnp.arange(0, MLA_ROPE, 2, dtype=F32) / MLA_ROPE)
    ang = positions.astype(F32)[:, None] * inv_freq
    return jnp.cos(ang).T, jnp.sin(ang).T


def kernel(x, p, positions, norm_mix, norm_mlp, norm_ple, mla_w_in, mla_q_norm, mla_kv_norm, mla_w_uq, mla_w_ukv, mla_q_gain, mla_k_gain, mla_w_o, sb_w_qkv, sb_w_o, mlp_w_up, mlp_w_down, ple_w_proj, ple_w_gate):
    assert x.shape[0] == 1, "one sequence per call"
    depth = p.shape[0]
    HD = SB_HEADS * SB_HEAD_DIM
    cosT, sinT = _rope_tables(positions[0])
    col = lambda v: v.astype(F32)[:, None]
    row = lambda v: v.astype(F32)[None, :]
    h = x[0]
    for i in range(depth):
        j = i // N_MIXERS
        if i % N_MIXERS == 0:
            qT, k, vT = _mla_proj(
                h, row(norm_mix[i]), mla_w_in[j].T.astype(BF16),
                col(mla_q_norm[j]), col(mla_kv_norm[j]),
                mla_w_uq[j].T.astype(BF16), mla_w_ukv[j].T.astype(BF16),
                col(mla_q_gain[j]), col(mla_k_gain[j]), cosT, sinT)
            o = _mla_attn(qT, k, vT)
            w_o = mla_w_o[j]
        else:
            w = sb_w_qkv[j]
            q, kT, v = _sb_proj(h, row(norm_mix[i]), w[:, :HD].astype(BF16),
                                w[:, HD:2 * HD].T.astype(BF16), w[:, 2 * HD:].astype(BF16))
            o = _sb_attn(q, kT, v)
            w_o = sb_w_o[j]
        h1, xn = _out_proj(o, h, w_o.astype(BF16), row(norm_mlp[i]))
        u = _mlp_up(xn, mlp_w_up[i].astype(BF16))
        h2 = _mlp_down(u, mlp_w_down[i].astype(BF16), h1)
        h = _ple(h2, row(norm_ple[i]), ple_w_gate[i].astype(BF16), p[i, 0], ple_w_proj[i].astype(BF16))
    return h[None]
```

```python
import functools

import jax
import jax.numpy as jnp
from jax import lax
from jax.experimental import pallas as pl
from jax.experimental.pallas import tpu as pltpu

F32 = jnp.float32
BF16 = jnp.bfloat16

CHUNK = 64
N_MIXERS = 2
MLA_HEADS = 8
MLA_Q_LORA = 1024
MLA_KV_LORA = 512
MLA_NOPE = 128
MLA_ROPE = 64
MLA_QK = MLA_NOPE + MLA_ROPE
MLA_QK_PAD = 256
MLA_V = 128
MLA_V_AUG = MLA_V + 16
ROPE_THETA = 10000.0
LOG2E = 1.4426950408889634
SB_HEADS = 4
SB_HEAD_DIM = 256
EPS = 1e-6

MIB = 1024 * 1024
V7X_VMEM_BYTES = 64 * MIB


def _params(semantics, vmem_mib):
    assert vmem_mib * MIB < V7X_VMEM_BYTES
    return pltpu.CompilerParams(dimension_semantics=semantics, vmem_limit_bytes=vmem_mib * MIB)


def _resident(shape):
    return pl.BlockSpec(shape, lambda *_: (0,) * len(shape), pipeline_mode=pl.Buffered(1))


def _rms_rows(x, gain):
    return x * lax.rsqrt(jnp.mean(x * x, axis=-1, keepdims=True) + EPS) * gain


def _neg_abs(x):
    bits = lax.bitcast_convert_type(x, jnp.uint32) | jnp.uint32(0x80000000)
    return lax.bitcast_convert_type(bits, F32)


def _rms_cols(x, gain):
    return x * lax.rsqrt(jnp.mean(x * x, axis=0, keepdims=True) + EPS) * gain


def _mla_proj_kernel(h_ref, g_ref, w_in_ref, qn_ref, kvn_ref, w_uq_ref, w_ukv_ref,
                     qg_ref, kg_ref, cos_ref, sin_ref, qT_ref, k_ref, vT_ref):
    tm = h_ref.shape[0]
    y = _rms_rows(h_ref[...], g_ref[...]).astype(BF16)
    projT = lax.dot_general(w_in_ref[...], y, (((1,), (1,)), ((), ())),
                            preferred_element_type=F32)
    cq = _rms_cols(projT[:MLA_Q_LORA], qn_ref[...]).astype(BF16)
    ckv = _rms_cols(projT[MLA_Q_LORA:MLA_Q_LORA + MLA_KV_LORA], kvn_ref[...]).astype(BF16)
    kr = projT[MLA_Q_LORA + MLA_KV_LORA:]
    qT = jnp.dot(w_uq_ref[...], cq, preferred_element_type=F32)
    kvT = jnp.dot(w_ukv_ref[...], ckv, preferred_element_type=F32)
    cos = cos_ref[...]
    sin = sin_ref[...]
    qg = qg_ref[...]
    kg = kg_ref[...]
    kr_ss = jnp.sum(kr * kr, axis=0, keepdims=True)
    scale = MLA_QK ** -0.5
    half = MLA_ROPE // 2
    pad = MLA_QK_PAD - MLA_QK
    aug = MLA_V_AUG - MLA_V
    ones_tile = (lax.broadcasted_iota(jnp.int32, (aug, tm), 0) == 0).astype(F32).astype(BF16)
    for hd in range(MLA_HEADS):
        qh = _rms_cols(qT[hd * MLA_QK:(hd + 1) * MLA_QK], qg) * scale
        x1 = qh[MLA_NOPE:MLA_NOPE + half]
        x2 = qh[MLA_NOPE + half:]
        qT_ref[hd, 0:MLA_NOPE, :] = qh[:MLA_NOPE].astype(BF16)
        qT_ref[hd, MLA_NOPE:MLA_NOPE + half, :] = (x1 * cos - x2 * sin).astype(BF16)
        qT_ref[hd, MLA_NOPE + half:MLA_QK, :] = (x2 * cos + x1 * sin).astype(BF16)
        qT_ref[hd, MLA_QK:, :] = jnp.zeros((pad, tm), BF16)

        base = hd * (MLA_NOPE + MLA_V)
        kn = kvT[base:base + MLA_NOPE]
        inv = lax.rsqrt((jnp.sum(kn * kn, axis=0, keepdims=True) + kr_ss) * (1.0 / MLA_QK) + EPS)
        kn = kn * inv * kg[:MLA_NOPE]
        krh = kr * inv * kg[MLA_NOPE:]
        k1 = krh[:half]
        k2 = krh[half:]
        kT = jnp.concatenate(
            [kn, k1 * cos - k2 * sin, k2 * cos + k1 * sin, jnp.zeros((pad, tm), F32)], axis=0)
        k_ref[hd] = kT.T.astype(BF16)
        vT_ref[hd, :MLA_V, :] = kvT[base + MLA_NOPE:base + MLA_NOPE + MLA_V].astype(BF16)
        vT_ref[hd, MLA_V:, :] = ones_tile


def _mla_proj(h, g, w_inT, qn, kvn, w_uqT, w_ukvT, qg, kg, cosT, sinT, tm=256):
    S, D = h.shape
    tm = min(tm, S)
    H = MLA_HEADS
    return pl.pallas_call(
        _mla_proj_kernel,
        grid=(S // tm,),
        in_specs=[
            pl.BlockSpec((tm, D), lambda i: (i, 0)),
            _resident((1, D)),
            _resident(w_inT.shape),
            _resident(qn.shape),
            _resident(kvn.shape),
            _resident(w_uqT.shape),
            _resident(w_ukvT.shape),
            _resident(qg.shape),
            _resident(kg.shape),
            pl.BlockSpec((MLA_ROPE // 2, tm), lambda i: (0, i)),
            pl.BlockSpec((MLA_ROPE // 2, tm), lambda i: (0, i)),
        ],
        out_specs=[
            pl.BlockSpec((H, MLA_QK_PAD, tm), lambda i: (0, 0, i)),
            pl.BlockSpec((H, tm, MLA_QK_PAD), lambda i: (0, i, 0)),
            pl.BlockSpec((H, MLA_V_AUG, tm), lambda i: (0, 0, i)),
        ],
        out_shape=[
            jax.ShapeDtypeStruct((H, MLA_QK_PAD, S), BF16),
            jax.ShapeDtypeStruct((H, S, MLA_QK_PAD), BF16),
            jax.ShapeDtypeStruct((H, MLA_V_AUG, S), BF16),
        ],
        compiler_params=_params(("parallel",), 52),
        name="mla_proj",
    )(h, g, w_inT, qn, kvn, w_uqT, w_ukvT, qg, kg, cosT, sinT)


def _mla_attn_kernel(qT_ref, k_ref, vT_ref, o_ref, sa_ref, sb_ref, pa_ref, pb_ref,
                     m_ref, alpha_ref, acc_ref, *, TK):
    i = pl.program_id(1)
    TQ = 2 * TK
    n_keys = k_ref.shape[1]

    def scores(t, s_ref):
        start = pl.multiple_of(jnp.minimum(t * TK, n_keys - TK), TK)
        s_ref[...] = jnp.dot(k_ref[0, pl.ds(start, TK), :], qT_ref[0],
                             preferred_element_type=F32)

    def pending_pv(t, p_ref):
        start = pl.multiple_of(jnp.maximum(t, 0) * TK, TK)
        pv = jnp.dot(vT_ref[0, :, pl.ds(start, TK)], p_ref[...], preferred_element_type=F32)
        acc_ref[...] = alpha_ref[...] * acc_ref[...] + pv

    def softmax(s_ref, p_ref, chunk_offset):
        if chunk_offset is not None:
            kc = lax.broadcasted_iota(jnp.int32, (TK, TQ), 0) // CHUNK + chunk_offset
            qc = lax.broadcasted_iota(jnp.int32, (TK, TQ), 1) // CHUNK
            s_ref[...] = jnp.where(kc <= qc, s_ref[...], -jnp.inf)
        m_prev = m_ref[...]
        m_new = jnp.maximum(m_prev, jnp.max(s_ref[...], axis=0, keepdims=True))
        m_ref[...] = m_new
        alpha_ref[...] = jnp.exp(m_prev - m_new)
        p_ref[...] = jnp.exp(s_ref[...] - m_ref[...]).astype(BF16)

    def pair(t, offsets):
        scores(t + 1, sb_ref)
        pending_pv(t - 1, pb_ref)
        softmax(sa_ref, pa_ref, offsets[0])
        scores(t + 2, sa_ref)
        pending_pv(t, pa_ref)
        softmax(sb_ref, pb_ref, offsets[1])

    m_ref[...] = jnp.full(m_ref.shape, -jnp.inf, F32)
    acc_ref[...] = jnp.zeros(acc_ref.shape, F32)
    alpha_ref[...] = jnp.ones(alpha_ref.shape, F32)
    pb_ref[...] = jnp.zeros(pb_ref.shape, BF16)
    scores(0, sa_ref)

    def body(j, carry):
        pair(2 * j, (None, None))
        return carry

    lax.fori_loop(0, i, body, 0)
    pair(2 * i, (0, TK // CHUNK))
    pending_pv(2 * i + 1, pb_ref)
    o = acc_ref[:MLA_V, :] / acc_ref[MLA_V:MLA_V + 1, :]
    o_ref[...] = o.T.astype(BF16)


def _mla_attn(qT, k, vT, tk=512):
    H, _, S = qT.shape
    tk = min(tk, S // 2)
    tq = 2 * tk
    return pl.pallas_call(
        functools.partial(_mla_attn_kernel, TK=tk),
        grid=(H, S // tq),
        in_specs=[
            pl.BlockSpec((1, MLA_QK_PAD, tq), lambda h, i: (h, 0, i)),
            pl.BlockSpec((1, S, MLA_QK_PAD), lambda h, i: (h, 0, 0)),
            pl.BlockSpec((1, MLA_V_AUG, S), lambda h, i: (h, 0, 0)),
        ],
        out_specs=pl.BlockSpec((tq, MLA_V), lambda h, i: (i, h)),
        out_shape=jax.ShapeDtypeStruct((S, H * MLA_V), BF16),
        scratch_shapes=[
            pltpu.VMEM((tk, tq), F32),
            pltpu.VMEM((tk, tq), F32),
            pltpu.VMEM((tk, tq), BF16),
            pltpu.VMEM((tk, tq), BF16),
            pltpu.VMEM((1, tq), F32),
            pltpu.VMEM((1, tq), F32),
            pltpu.VMEM((MLA_V_AUG, tq), F32),
        ],
        compiler_params=_params(("parallel", "arbitrary"), 48),
        name="mla_attn",
    )(qT, k, vT)


def _sb_proj_kernel(h_ref, g_ref, wq_ref, wkT_ref, wv_ref, q_ref, kT_ref, v_ref):
    y = _rms_rows(h_ref[...], g_ref[...]).astype(BF16)
    q = jnp.dot(y, wq_ref[...], preferred_element_type=F32)
    q_ref[...] = (q * (SB_HEAD_DIM ** -0.5)).astype(BF16)
    v_ref[...] = jnp.dot(y, wv_ref[...], preferred_element_type=F32).astype(BF16)
    kT = lax.dot_general(wkT_ref[...], y, (((1,), (1,)), ((), ())),
                         preferred_element_type=F32)
    for hd in range(SB_HEADS):
        kT_ref[hd] = kT[hd * SB_HEAD_DIM:(hd + 1) * SB_HEAD_DIM].astype(BF16)


def _sb_proj(h, g, wq, wkT, wv, tm=256):
    S, D = h.shape
    tm = min(tm, S)
    HD = SB_HEADS * SB_HEAD_DIM
    return pl.pallas_call(
        _sb_proj_kernel,
        grid=(S // tm,),
        in_specs=[
            pl.BlockSpec((tm, D), lambda i: (i, 0)),
            _resident((1, D)),
            _resident(wq.shape),
            _resident(wkT.shape),
            _resident(wv.shape),
        ],
        out_specs=[
            pl.BlockSpec((tm, HD), lambda i: (i, 0)),
            pl.BlockSpec((SB_HEADS, SB_HEAD_DIM, tm), lambda i: (0, 0, i)),
            pl.BlockSpec((tm, HD), lambda i: (i, 0)),
        ],
        out_shape=[
            jax.ShapeDtypeStruct((S, HD), BF16),
            jax.ShapeDtypeStruct((SB_HEADS, SB_HEAD_DIM, S), BF16),
            jax.ShapeDtypeStruct((S, HD), BF16),
        ],
        compiler_params=_params(("parallel",), 52),
        name="sb_proj",
    )(h, g, wq, wkT, wv)


def _sb_attn_kernel(q_ref, kT_ref, v_ref, o_ref, za_ref, zb_ref, aa_ref, ab_ref,
                    acc_ref, after_ref, later_ref, *, TK):
    i = pl.program_id(1)
    row = lax.broadcasted_iota(jnp.int32, (TK, TK), 0)
    col = lax.broadcasted_iota(jnp.int32, (TK, TK), 1)
    later_ref[...] = (row > col).astype(F32).astype(BF16)
    last = 2 * i + 1

    def key_start(t):
        return pl.multiple_of(jnp.maximum(last - t, 0) * TK, TK)

    def logits(t, z_ref):
        z_ref[...] = jnp.dot(q_ref[...], kT_ref[0, :, pl.ds(key_start(t), TK)],
                             preferred_element_type=F32)

    def pending_av(t, a_ref):
        acc_ref[...] += jnp.dot(a_ref[...], v_ref[pl.ds(key_start(jnp.maximum(t, 0)), TK), :],
                                preferred_element_type=F32)

    def weights(z_ref, a_ref, mask):
        for r in range(2):
            rows = slice(r * TK, (r + 1) * TK)
            kind = None if mask is None else mask[r]
            if kind == "none":
                a_ref[rows, :] = jnp.zeros((TK, TK), BF16)
                continue
            z = z_ref[rows, :] * LOG2E
            sp = jnp.maximum(z, 0.0) + jnp.log2(1.0 + jnp.exp2(_neg_abs(z)))
            if kind == "diag":
                valid = col < row
                sp = jnp.where(valid, sp, 0.0)
            within = jnp.dot(sp.astype(BF16), later_ref[...], preferred_element_type=F32)
            after = after_ref[rows, :]
            a = jnp.exp2(z - sp - within - after)
            if kind == "diag":
                a = jnp.where(valid, a, 0.0)
            a_ref[rows, :] = a.astype(BF16)
            after_ref[rows, :] = after + jnp.sum(sp, axis=1, keepdims=True)

    def pair(t, masks):
        logits(t + 1, zb_ref)
        pending_av(t - 1, ab_ref)
        weights(za_ref, aa_ref, masks[0])
        logits(t + 2, za_ref)
        pending_av(t, aa_ref)
        weights(zb_ref, ab_ref, masks[1])

    acc_ref[...] = jnp.zeros(acc_ref.shape, F32)
    after_ref[...] = jnp.zeros(after_ref.shape, F32)
    ab_ref[...] = jnp.zeros(ab_ref.shape, BF16)
    logits(0, za_ref)
    pair(0, (("none", "diag"), ("diag", None)))

    def body(j, carry):
        pair(2 * j + 2, (None, None))
        return carry

    lax.fori_loop(0, i, body, 0)
    pending_av(last, ab_ref)
    o_ref[...] = acc_ref[...].astype(BF16)


def _sb_attn(q, kT, v, tk=256):
    S = q.shape[0]
    tk = min(tk, S // 2)
    tq = 2 * tk
    Dh = SB_HEAD_DIM
    return pl.pallas_call(
        functools.partial(_sb_attn_kernel, TK=tk),
        grid=(SB_HEADS, S // tq),
        in_specs=[
            pl.BlockSpec((tq, Dh), lambda h, i: (i, h)),
            pl.BlockSpec((1, Dh, S), lambda h, i: (h, 0, 0)),
            pl.BlockSpec((S, Dh), lambda h, i: (0, h)),
        ],
        out_specs=pl.BlockSpec((tq, Dh), lambda h, i: (i, h)),
        out_shape=jax.ShapeDtypeStruct((S, SB_HEADS * Dh), BF16),
        scratch_shapes=[
            pltpu.VMEM((tq, tk), F32),
            pltpu.VMEM((tq, tk), F32),
            pltpu.VMEM((tq, tk), BF16),
            pltpu.VMEM((tq, tk), BF16),
            pltpu.VMEM((tq, Dh), F32),
            pltpu.VMEM((tq, 1), F32),
            pltpu.VMEM((tk, tk), BF16),
        ],
        compiler_params=_params(("parallel", "arbitrary"), 48),
        name="sb_attn",
    )(q, kT, v)


def _out_proj_kernel(o_ref, h_ref, w_ref, g_ref, h1_ref, xn_ref):
    h1 = h_ref[...] + jnp.dot(o_ref[...], w_ref[...], preferred_element_type=F32)
    h1_ref[...] = h1
    xn_ref[...] = _rms_rows(h1, g_ref[...]).astype(BF16)


def _out_proj(o, h, w, g, tm=256):
    S, D = h.shape
    tm = min(tm, S)
    return pl.pallas_call(
        _out_proj_kernel,
        grid=(S // tm,),
        in_specs=[
            pl.BlockSpec((tm, o.shape[1]), lambda i: (i, 0)),
            pl.BlockSpec((tm, D), lambda i: (i, 0)),
            _resident(w.shape),
            _resident((1, D)),
        ],
        out_specs=[
            pl.BlockSpec((tm, D), lambda i: (i, 0)),
            pl.BlockSpec((tm, D), lambda i: (i, 0)),
        ],
        out_shape=[
            jax.ShapeDtypeStruct((S, D), F32),
            jax.ShapeDtypeStruct((S, D), BF16),
        ],
        compiler_params=_params(("parallel",), 48),
        name="out_proj",
    )(o, h, w, g)


def _mlp_up_kernel(x_ref, w_ref, u_ref):
    u = jnp.maximum(jnp.dot(x_ref[...], w_ref[...], preferred_element_type=F32), 0.0)
    u_ref[...] = (u * u).astype(BF16)


def _mlp_up(xn, w, tm=1024, tn=1024):
    S, D = xn.shape
    F = w.shape[1]
    tm, tn = min(tm, S), min(tn, F)
    return pl.pallas_call(
        _mlp_up_kernel,
        grid=(S // tm, F // tn),
        in_specs=[
            pl.BlockSpec((tm, D), lambda i, j: (i, 0)),
            pl.BlockSpec((D, tn), lambda i, j: (0, j)),
        ],
        out_specs=pl.BlockSpec((tm, tn), lambda i, j: (i, j)),
        out_shape=jax.ShapeDtypeStruct((S, F), BF16),
        compiler_params=_params(("parallel", "arbitrary"), 52),
        name="mlp_up",
    )(xn, w)


def _mlp_down_kernel(u_ref, w_ref, r_ref, o_ref):
    @pl.when(pl.program_id(2) == 0)
    def _():
        o_ref[...] = r_ref[...]

    o_ref[...] += jnp.dot(u_ref[...], w_ref[...], preferred_element_type=F32)


def _mlp_down(u, w, res, tm=1024, tn=2048, tk=1024):
    S, F = u.shape
    D = w.shape[1]
    tm, tn, tk = min(tm, S), min(tn, D), min(tk, F)
    return pl.pallas_call(
        _mlp_down_kernel,
        grid=(S // tm, D // tn, F // tk),
        in_specs=[
            pl.BlockSpec((tm, tk), lambda i, j, k: (i, k)),
            pl.BlockSpec((tk, tn), lambda i, j, k: (k, j)),
            pl.BlockSpec((tm, tn), lambda i, j, k: (i, j)),
        ],
        out_specs=pl.BlockSpec((tm, tn), lambda i, j, k: (i, j)),
        out_shape=jax.ShapeDtypeStruct((S, D), F32),
        compiler_params=_params(("parallel", "parallel", "arbitrary"), 52),
        name="mlp_down",
    )(u, w, res)


def _ple_kernel(hrow_ref, hcol_ref, g_ref, wg_ref, p_ref, wp_ref, o_ref, xn_ref):
    @pl.when(pl.program_id(1) == 0)
    def _():
        xn_ref[...] = _rms_rows(hrow_ref[...], g_ref[...]).astype(BF16)

    logits = jnp.dot(xn_ref[...], wg_ref[...], preferred_element_type=F32)
    gate = 1.0 / (1.0 + jnp.exp(-logits))
    emb = jnp.dot(p_ref[...].astype(BF16), wp_ref[...], preferred_element_type=F32)
    o_ref[...] = hcol_ref[...] + emb * gate


def _ple(h, g, wg, p, wp, tm=512, tn=512):
    S, D = h.shape
    P = p.shape[1]
    tm, tn = min(tm, S), min(tn, D)
    return pl.pallas_call(
        _ple_kernel,
        grid=(S // tm, D // tn),
        in_specs=[
            pl.BlockSpec((tm, D), lambda i, j: (i, 0)),
            pl.BlockSpec((tm, tn), lambda i, j: (i, j)),
            _resident((1, D)),
            pl.BlockSpec((D, tn), lambda i, j: (0, j)),
            pl.BlockSpec((tm, P), lambda i, j: (i, 0)),
            pl.BlockSpec((P, tn), lambda i, j: (0, j)),
        ],
        out_specs=pl.BlockSpec((tm, tn), lambda i, j: (i, j)),
        out_shape=jax.ShapeDtypeStruct((S, D), F32),
        scratch_shapes=[pltpu.VMEM((tm, D), BF16)],
        compiler_params=_params(("parallel", "arbitrary"), 52),
        name="ple_gate",
    )(h, h, g, wg, p, wp)


def _rope_tables(positions):
    inv_freq = ROPE_THETA ** (-jnp.arange(0, MLA_ROPE, 2, dtype=F32) / MLA_ROPE)
    ang = positions.astype(F32)[:, None] * inv_freq
    return jnp.cos(ang).T, jnp.sin(ang).T


def kernel(x, p, positions, norm_mix, norm_mlp, norm_ple, mla_w_in, mla_q_norm, mla_kv_norm, mla_w_uq, mla_w_ukv, mla_q_gain, mla_k_gain, mla_w_o, sb_w_qkv, sb_w_o, mlp_w_up, mlp_w_down, ple_w_proj, ple_w_gate):
    assert x.shape[0] == 1, "one sequence per call"
    depth = p.shape[0]
    HD = SB_HEADS * SB_HEAD_DIM
    cosT, sinT = _rope_tables(positions[0])
    col = lambda v: v.astype(F32)[:, None]
    row = lambda v: v.astype(F32)[None, :]
    h = x[0]
    for i in range(depth):
        j = i // N_MIXERS
        if i % N_MIXERS == 0:
            qT, k, vT = _mla_proj(
                h, row(norm_mix[i]), mla_w_in[j].T.astype(BF16),
                col(mla_q_norm[j]), col(mla_kv_norm[j]),
                mla_w_uq[j].T.astype(BF16), mla_w_ukv[j].T.astype(BF16),
                col(mla_q_gain[j]), col(mla_k_gain[j]), cosT, sinT)
            o = _mla_attn(qT, k, vT)
            w_o = mla_w_o[j]
        else:
            w = sb_w_qkv[j]
            q, kT, v = _sb_proj(h, row(norm_mix[i]), w[:, :HD].astype(BF16),
                                w[:, HD:2 * HD].T.astype(BF16), w[:, 2 * HD:].astype(BF16))
            o = _sb_attn(q, kT, v)
            w_o = sb_w_o[j]
        h1, xn = _out_proj(o, h, w_o.astype(BF16), row(norm_mlp[i]))
        u = _mlp_up(xn, mlp_w_up[i].astype(BF16))
        h2 = _mlp_down(u, mlp_w_down[i].astype(BF16), h1)
        h = _ple(h2, row(norm_ple[i]), ple_w_gate[i].astype(BF16), p[i, 0], ple_w_proj[i].astype(BF16))
    return h[None]
```

```python
import functools

import jax
import jax.numpy as jnp
from jax import lax
from jax.experimental import pallas as pl
from jax.experimental.pallas import tpu as pltpu

F32 = jnp.float32
BF16 = jnp.bfloat16

CHUNK = 64
N_MIXERS = 2
MLA_HEADS = 8
MLA_Q_LORA = 1024
MLA_KV_LORA = 512
MLA_NOPE = 128
MLA_ROPE = 64
MLA_QK = MLA_NOPE + MLA_ROPE
MLA_QK_PAD = 256
MLA_V = 128
MLA_V_AUG = MLA_V + 16
ROPE_THETA = 10000.0
LOG2E = 1.4426950408889634
SB_HEADS = 4
SB_HEAD_DIM = 256
EPS = 1e-6

MIB = 1024 * 1024
V7X_VMEM_BYTES = 64 * MIB
LANES = 128


def _params(semantics, vmem_mib):
    assert vmem_mib * MIB < V7X_VMEM_BYTES
    return pltpu.CompilerParams(dimension_semantics=semantics, vmem_limit_bytes=vmem_mib * MIB)


def _resident(shape):
    return pl.BlockSpec(shape, lambda *_: (0,) * len(shape), pipeline_mode=pl.Buffered(1))


def _rms_rows(x, gain):
    return x * lax.rsqrt(jnp.mean(x * x, axis=-1, keepdims=True) + EPS) * gain


def _neg_abs(x):
    bits = lax.bitcast_convert_type(x, jnp.uint32) | jnp.uint32(0x80000000)
    return lax.bitcast_convert_type(bits, F32)


def _rms_cols(x, gain):
    return x * lax.rsqrt(jnp.mean(x * x, axis=0, keepdims=True) + EPS) * gain


def _mla_proj_kernel(h_ref, g_ref, w_in_ref, qn_ref, kvn_ref, w_uq_ref, w_ukv_ref,
                     qg_ref, kg_ref, cos_ref, sin_ref, qT_ref, k_ref, vT_ref):
    tm = h_ref.shape[0]
    y = _rms_rows(h_ref[...], g_ref[...]).astype(BF16)
    projT = lax.dot_general(w_in_ref[...], y, (((1,), (1,)), ((), ())),
                            preferred_element_type=F32)
    cq = _rms_cols(projT[:MLA_Q_LORA], qn_ref[...]).astype(BF16)
    ckv = _rms_cols(projT[MLA_Q_LORA:MLA_Q_LORA + MLA_KV_LORA], kvn_ref[...]).astype(BF16)
    kr = projT[MLA_Q_LORA + MLA_KV_LORA:]
    qT = jnp.dot(w_uq_ref[...], cq, preferred_element_type=F32)
    kvT = jnp.dot(w_ukv_ref[...], ckv, preferred_element_type=F32)
    cos = cos_ref[...]
    sin = sin_ref[...]
    qg = qg_ref[...]
    kg = kg_ref[...]
    kr_ss = jnp.sum(kr * kr, axis=0, keepdims=True)
    scale = MLA_QK ** -0.5 * LOG2E
    half = MLA_ROPE // 2
    pad = MLA_QK_PAD - MLA_QK
    aug = MLA_V_AUG - MLA_V
    ones_tile = (lax.broadcasted_iota(jnp.int32, (aug, tm), 0) == 0).astype(F32).astype(BF16)
    for hd in range(MLA_HEADS):
        qh = _rms_cols(qT[hd * MLA_QK:(hd + 1) * MLA_QK], qg) * scale
        x1 = qh[MLA_NOPE:MLA_NOPE + half]
        x2 = qh[MLA_NOPE + half:]
        qT_ref[hd, 0:MLA_NOPE, :] = qh[:MLA_NOPE].astype(BF16)
        qT_ref[hd, MLA_NOPE:MLA_NOPE + half, :] = (x1 * cos - x2 * sin).astype(BF16)
        qT_ref[hd, MLA_NOPE + half:MLA_QK, :] = (x2 * cos + x1 * sin).astype(BF16)
        qT_ref[hd, MLA_QK:, :] = jnp.zeros((pad, tm), BF16)

        base = hd * (MLA_NOPE + MLA_V)
        kn = kvT[base:base + MLA_NOPE]
        inv = lax.rsqrt((jnp.sum(kn * kn, axis=0, keepdims=True) + kr_ss) * (1.0 / MLA_QK) + EPS)
        kn = kn * inv * kg[:MLA_NOPE]
        krh = kr * inv * kg[MLA_NOPE:]
        k1 = krh[:half]
        k2 = krh[half:]
        kT = jnp.concatenate(
            [kn, k1 * cos - k2 * sin, k2 * cos + k1 * sin, jnp.zeros((pad, tm), F32)], axis=0)
        k_ref[hd] = kT.T.astype(BF16)
        vT_ref[hd, :MLA_V, :] = kvT[base + MLA_NOPE:base + MLA_NOPE + MLA_V].astype(BF16)
        vT_ref[hd, MLA_V:, :] = ones_tile


def _mla_proj(h, g, w_inT, qn, kvn, w_uqT, w_ukvT, qg, kg, cosT, sinT, tm=256):
    S, D = h.shape
    tm = min(tm, S)
    H = MLA_HEADS
    return pl.pallas_call(
        _mla_proj_kernel,
        grid=(S // tm,),
        in_specs=[
            pl.BlockSpec((tm, D), lambda i: (i, 0)),
            _resident((1, D)),
            _resident(w_inT.shape),
            _resident(qn.shape),
            _resident(kvn.shape),
            _resident(w_uqT.shape),
            _resident(w_ukvT.shape),
            _resident(qg.shape),
            _resident(kg.shape),
            pl.BlockSpec((MLA_ROPE // 2, tm), lambda i: (0, i)),
            pl.BlockSpec((MLA_ROPE // 2, tm), lambda i: (0, i)),
        ],
        out_specs=[
            pl.BlockSpec((H, MLA_QK_PAD, tm), lambda i: (0, 0, i)),
            pl.BlockSpec((H, tm, MLA_QK_PAD), lambda i: (0, i, 0)),
            pl.BlockSpec((H, MLA_V_AUG, tm), lambda i: (0, 0, i)),
        ],
        out_shape=[
            jax.ShapeDtypeStruct((H, MLA_QK_PAD, S), BF16),
            jax.ShapeDtypeStruct((H, S, MLA_QK_PAD), BF16),
            jax.ShapeDtypeStruct((H, MLA_V_AUG, S), BF16),
        ],
        compiler_params=_params(("parallel",), 52),
        name="mla_proj",
    )(h, g, w_inT, qn, kvn, w_uqT, w_ukvT, qg, kg, cosT, sinT)


MLA_DEPTH = 4


def _mla_attn_kernel(qT_ref, k_ref, vT_ref, o_ref, s0_ref, s1_ref, s2_ref, s3_ref, p0_ref, p1_ref,
                     m_ref, alpha_ref, acc_ref, *, TK):
    i = pl.program_id(1)
    N = MLA_DEPTH
    TQ = N * TK
    s_refs = (s0_ref, s1_ref, s2_ref, s3_ref)
    p_refs = (p0_ref, p1_ref)

    def scores(block, u):
        start = pl.multiple_of(block * TK, TK)
        s_refs[u % N][...] = jnp.dot(k_ref[0, pl.ds(start, TK), :], qT_ref[0],
                                     preferred_element_type=F32)

    def pv(block, u):
        start = pl.multiple_of(block * TK, TK)
        prod = jnp.dot(vT_ref[0, :, pl.ds(start, TK)], p_refs[u % 2][...],
                       preferred_element_type=F32)
        acc_ref[...] = alpha_ref[u % N] * acc_ref[...] + prod

    def running_max(u, chunk_offset):
        s_ref = s_refs[u % N]
        if chunk_offset is not None:
            kc = lax.broadcasted_iota(jnp.int32, (TK, TQ), 0) // CHUNK + chunk_offset
            qc = lax.broadcasted_iota(jnp.int32, (TK, TQ), 1) // CHUNK
            s_ref[...] = jnp.where(kc <= qc, s_ref[...], -jnp.inf)
        m_prev = m_ref[(u - 1) % N]
        m_new = jnp.maximum(m_prev, jnp.max(s_ref[...], axis=0, keepdims=True))
        m_ref[u % N] = m_new
        alpha_ref[u % N] = jnp.exp2(m_prev - m_new)

    def probabilities(u):
        p_refs[u % 2][...] = jnp.exp2(s_refs[u % N][...] - m_ref[u % N]).astype(BF16)

    def step(u, block_ahead, block_behind, offset_next):
        scores(block_ahead, u + 2)
        pv(block_behind, u - 1)
        probabilities(u)
        running_max(u + 1, offset_next)

    first = N * i
    per_block = TK // CHUNK
    acc_ref[...] = jnp.zeros(acc_ref.shape, F32)
    p_refs[1][...] = jnp.zeros(p_refs[1].shape, BF16)
    alpha_ref[N - 1] = jnp.ones(alpha_ref.shape[1:], F32)
    m_ref[N - 1] = jnp.full(m_ref.shape[1:], -jnp.inf, F32)
    scores(first, 0)
    scores(first + 1, 1)
    running_max(0, 0)

    step(0, first + 2, first, per_block)
    step(1, first + 3, first, 2 * per_block)
    step(2, 0, first + 1, 3 * per_block)
    step(3, 1, first + 2, None)

    def body(j, carry):
        base = N * j
        step(4, base + 2, jnp.where(j == 0, first + 3, base - 1), None)
        step(5, base + 3, base, None)
        step(6, base + 4, base + 1, None)
        step(7, base + 5, base + 2, None)
        return carry

    lax.fori_loop(0, i, body, 0)
    pv(jnp.where(i == 0, first + 3, first - 1), N - 1)
    o = acc_ref[:MLA_V, :] / acc_ref[MLA_V:MLA_V + 1, :]
    o_ref[...] = o.T.astype(BF16)


def _mla_attn(qT, k, vT, tk=512):
    H, _, S = qT.shape
    tk = min(tk, S // MLA_DEPTH)
    tq = MLA_DEPTH * tk
    return pl.pallas_call(
        functools.partial(_mla_attn_kernel, TK=tk),
        grid=(H, S // tq),
        in_specs=[
            pl.BlockSpec((1, MLA_QK_PAD, tq), lambda h, i: (h, 0, i)),
            pl.BlockSpec((1, S, MLA_QK_PAD), lambda h, i: (h, 0, 0)),
            pl.BlockSpec((1, MLA_V_AUG, S), lambda h, i: (h, 0, 0)),
        ],
        out_specs=pl.BlockSpec((tq, MLA_V), lambda h, i: (i, h)),
        out_shape=jax.ShapeDtypeStruct((S, H * MLA_V), BF16),
        scratch_shapes=(
            [pltpu.VMEM((tk, tq), F32)] * MLA_DEPTH
            + [pltpu.VMEM((tk, tq), BF16)] * 2
            + [pltpu.VMEM((MLA_DEPTH, 1, tq), F32),
               pltpu.VMEM((MLA_DEPTH, 1, tq), F32),
               pltpu.VMEM((MLA_V_AUG, tq), F32)]
        ),
        compiler_params=_params(("parallel", "arbitrary"), 54),
        name="mla_attn",
    )(qT, k, vT)


def _sb_proj_kernel(h_ref, g_ref, w_ref, q_ref, kT_ref, v_ref):
    HD = SB_HEADS * SB_HEAD_DIM
    y = _rms_rows(h_ref[...], g_ref[...]).astype(BF16)
    qkv = jnp.dot(y, w_ref[...], preferred_element_type=F32)
    q_ref[...] = (qkv[:, :HD] * (SB_HEAD_DIM ** -0.5 * LOG2E)).astype(BF16)
    v_ref[...] = qkv[:, 2 * HD:].astype(BF16)
    for hd in range(SB_HEADS):
        k = qkv[:, HD + hd * SB_HEAD_DIM:HD + (hd + 1) * SB_HEAD_DIM]
        kT_ref[hd] = k.T.astype(BF16)


def _sb_proj(h, g, w, tm=256):
    S, D = h.shape
    tm = min(tm, S)
    HD = SB_HEADS * SB_HEAD_DIM
    return pl.pallas_call(
        _sb_proj_kernel,
        grid=(S // tm,),
        in_specs=[
            pl.BlockSpec((tm, D), lambda i: (i, 0)),
            _resident((1, D)),
            _resident(w.shape),
        ],
        out_specs=[
            pl.BlockSpec((tm, HD), lambda i: (i, 0)),
            pl.BlockSpec((SB_HEADS, SB_HEAD_DIM, tm), lambda i: (0, 0, i)),
            pl.BlockSpec((tm, HD), lambda i: (i, 0)),
        ],
        out_shape=[
            jax.ShapeDtypeStruct((S, HD), BF16),
            jax.ShapeDtypeStruct((SB_HEADS, SB_HEAD_DIM, S), BF16),
            jax.ShapeDtypeStruct((S, HD), BF16),
        ],
        compiler_params=_params(("parallel",), 52),
        name="sb_proj",
    )(h, g, w)


def _sb_attn_kernel(q_ref, kT_ref, v_ref, o_ref, za_ref, zb_ref, aa_ref, ab_ref,
                    acc_ref, after_ref, later_ref, *, TK, R):
    i = pl.program_id(1)
    row = lax.broadcasted_iota(jnp.int32, (TK, TK), 0)
    col = lax.broadcasted_iota(jnp.int32, (TK, TK), 1)
    later_ref[...] = (row > col).astype(F32).astype(BF16)
    last = R * i + R - 1

    def key_start(t):
        return pl.multiple_of(jnp.maximum(last - t, 0) * TK, TK)

    def logits(t, z_ref):
        z_ref[...] = jnp.dot(q_ref[...], kT_ref[0, :, pl.ds(key_start(t), TK)],
                             preferred_element_type=F32)

    def pending_av(t, a_ref):
        acc_ref[...] += jnp.dot(a_ref[...], v_ref[pl.ds(key_start(jnp.maximum(t, 0)), TK), :],
                                preferred_element_type=F32)

    def weights(z_ref, a_ref, mask):
        for r in range(R):
            rows = slice(r * TK, (r + 1) * TK)
            kind = None if mask is None else mask[r]
            if kind == "none":
                a_ref[rows, :] = jnp.zeros((TK, TK), BF16)
                continue
            z = z_ref[rows, :]
            sp = jnp.maximum(z, 0.0) + jnp.log2(1.0 + jnp.exp2(_neg_abs(z)))
            if kind == "diag":
                valid = col < row
                sp = jnp.where(valid, sp, 0.0)
            within = jnp.dot(sp.astype(BF16), later_ref[...], preferred_element_type=F32)
            after = after_ref[rows, :]
            a = jnp.exp2(z - sp - within - after)
            if kind == "diag":
                a = jnp.where(valid, a, 0.0)
            a_ref[rows, :] = a.astype(BF16)
            after_ref[rows, :] = after + jnp.sum(sp, axis=1, keepdims=True)

    def pair(t, masks):
        logits(t + 1, zb_ref)
        pending_av(t - 1, ab_ref)
        weights(za_ref, aa_ref, masks[0])
        logits(t + 2, za_ref)
        pending_av(t, aa_ref)
        weights(zb_ref, ab_ref, masks[1])

    acc_ref[...] = jnp.zeros(acc_ref.shape, F32)
    after_ref[...] = jnp.zeros(after_ref.shape, F32)
    ab_ref[...] = jnp.zeros(ab_ref.shape, BF16)
    logits(0, za_ref)

    def diagonal_mask(t):
        beside = R - 1 - t
        return tuple("diag" if r == beside else ("none" if r < beside else None) for r in range(R))

    for t in range(0, R, 2):
        pair(t, (diagonal_mask(t), diagonal_mask(t + 1)))

    def body(j, carry):
        pair(2 * j + R, (None, None))
        return carry

    lax.fori_loop(0, (R // 2) * i, body, 0)
    pending_av(last, ab_ref)
    o_ref[...] = acc_ref[...].astype(BF16)


def _sb_attn(q, kT, v, tk=256, r=4):
    S = q.shape[0]
    tk = min(tk, S // r)
    tq = r * tk
    Dh = SB_HEAD_DIM
    return pl.pallas_call(
        functools.partial(_sb_attn_kernel, TK=tk, R=r),
        grid=(SB_HEADS, S // tq),
        in_specs=[
            pl.BlockSpec((tq, Dh), lambda h, i: (i, h)),
            pl.BlockSpec((1, Dh, S), lambda h, i: (h, 0, 0)),
            pl.BlockSpec((S, Dh), lambda h, i: (0, h)),
        ],
        out_specs=pl.BlockSpec((tq, Dh), lambda h, i: (i, h)),
        out_shape=jax.ShapeDtypeStruct((S, SB_HEADS * Dh), BF16),
        scratch_shapes=[
            pltpu.VMEM((tq, tk), F32),
            pltpu.VMEM((tq, tk), F32),
            pltpu.VMEM((tq, tk), BF16),
            pltpu.VMEM((tq, tk), BF16),
            pltpu.VMEM((tq, Dh), F32),
            pltpu.VMEM((tq, 1), F32),
            pltpu.VMEM((tk, tk), BF16),
        ],
        compiler_params=_params(("parallel", "arbitrary"), 48),
        name="sb_attn",
    )(q, kT, v)


def _out_proj_kernel(o_ref, h_ref, w_ref, g_ref, h1_ref, xn_ref):
    h1 = h_ref[...] + jnp.dot(o_ref[...], w_ref[...], preferred_element_type=F32)
    h1_ref[...] = h1
    xn_ref[...] = _rms_rows(h1, g_ref[...]).astype(BF16)


def _out_proj(o, h, w, g, tm=256):
    S, D = h.shape
    tm = min(tm, S)
    return pl.pallas_call(
        _out_proj_kernel,
        grid=(S // tm,),
        in_specs=[
            pl.BlockSpec((tm, o.shape[1]), lambda i: (i, 0)),
            pl.BlockSpec((tm, D), lambda i: (i, 0)),
            _resident(w.shape),
            _resident((1, D)),
        ],
        out_specs=[
            pl.BlockSpec((tm, D), lambda i: (i, 0)),
            pl.BlockSpec((tm, D), lambda i: (i, 0)),
        ],
        out_shape=[
            jax.ShapeDtypeStruct((S, D), F32),
            jax.ShapeDtypeStruct((S, D), BF16),
        ],
        compiler_params=_params(("parallel",), 48),
        name="out_proj",
    )(o, h, w, g)


def _mlp_up_kernel(x_ref, w_ref, u_ref):
    u = jnp.dot(x_ref[...], w_ref[...].astype(BF16), preferred_element_type=F32)
    u = jnp.maximum(u, 0.0)
    u_ref[...] = (u * u).astype(BF16)


def _mlp_up(xn, w, tm=1024, tn=512):
    S, D = xn.shape
    F = w.shape[1]
    tm, tn = min(tm, S), min(tn, F)
    return pl.pallas_call(
        _mlp_up_kernel,
        grid=(S // tm, F // tn),
        in_specs=[
            pl.BlockSpec((tm, D), lambda i, j: (i, 0)),
            pl.BlockSpec((D, tn), lambda i, j: (0, j)),
        ],
        out_specs=pl.BlockSpec((tm, tn), lambda i, j: (i, j)),
        out_shape=jax.ShapeDtypeStruct((S, F), BF16),
        compiler_params=_params(("parallel", "arbitrary"), 52),
        name="mlp_up",
    )(xn, w)


def _mlp_down_kernel(u_ref, w_ref, r_ref, g_ref, o_ref, hg_ref, ss_ref):
    k = pl.program_id(2)

    @pl.when(k == 0)
    def _():
        o_ref[...] = r_ref[...]

    o_ref[...] += jnp.dot(u_ref[...], w_ref[...], preferred_element_type=F32)

    @pl.when(k == pl.num_programs(2) - 1)
    def _():
        o = o_ref[...]
        hg_ref[...] = (o * g_ref[...]).astype(BF16)
        ss_ref[...] = jnp.broadcast_to(jnp.sum(o * o, axis=-1, keepdims=True), ss_ref.shape)


def _mlp_down(u, w, res, g, tm=1024, tn=1024, tk=2048):
    S, F = u.shape
    D = w.shape[1]
    tm, tn, tk = min(tm, S), min(tn, D), min(tk, F)
    return pl.pallas_call(
        _mlp_down_kernel,
        grid=(S // tm, D // tn, F // tk),
        in_specs=[
            pl.BlockSpec((tm, tk), lambda i, j, k: (i, k)),
            pl.BlockSpec((tk, tn), lambda i, j, k: (k, j)),
            pl.BlockSpec((tm, tn), lambda i, j, k: (i, j)),
            pl.BlockSpec((1, tn), lambda i, j, k: (0, j)),
        ],
        out_specs=[
            pl.BlockSpec((tm, tn), lambda i, j, k: (i, j)),
            pl.BlockSpec((tm, tn), lambda i, j, k: (i, j)),
            pl.BlockSpec((tm, LANES), lambda i, j, k: (i, j)),
        ],
        out_shape=[
            jax.ShapeDtypeStruct((S, D), F32),
            jax.ShapeDtypeStruct((S, D), BF16),
            jax.ShapeDtypeStruct((S, (D // tn) * LANES), F32),
        ],
        compiler_params=_params(("parallel", "parallel", "arbitrary"), 48),
        name="mlp_down",
    )(u, w, res, g)


def _ple_kernel(hg_ref, ss_ref, h_ref, wg_ref, p_ref, wp_ref, o_ref, *, d_model):
    ss = ss_ref[...]
    total = ss[:, 0:1]
    for t in range(1, ss.shape[1] // LANES):
        total = total + ss[:, t * LANES:t * LANES + 1]
    inv = lax.rsqrt(total * (1.0 / d_model) + EPS)
    logits = jnp.dot(hg_ref[...], wg_ref[...], preferred_element_type=F32) * inv
    gate = 1.0 / (1.0 + jnp.exp(-logits))
    emb = jnp.dot(p_ref[...].astype(BF16), wp_ref[...], preferred_element_type=F32)
    o_ref[...] = h_ref[...] + emb * gate


def _ple(h, hg, ss, wg, p, wp, tm=1024, tn=512):
    S, D = h.shape
    P = p.shape[1]
    tm, tn = min(tm, S), min(tn, D)
    return pl.pallas_call(
        functools.partial(_ple_kernel, d_model=D),
        grid=(S // tm, D // tn),
        in_specs=[
            pl.BlockSpec((tm, D), lambda i, j: (i, 0)),
            pl.BlockSpec((tm, ss.shape[1]), lambda i, j: (i, 0)),
            pl.BlockSpec((tm, tn), lambda i, j: (i, j)),
            pl.BlockSpec((D, tn), lambda i, j: (0, j)),
            pl.BlockSpec((tm, P), lambda i, j: (i, 0)),
            pl.BlockSpec((P, tn), lambda i, j: (0, j)),
        ],
        out_specs=pl.BlockSpec((tm, tn), lambda i, j: (i, j)),
        out_shape=jax.ShapeDtypeStruct((S, D), F32),
        compiler_params=_params(("parallel", "arbitrary"), 48),
        name="ple_gate",
    )(hg, ss, h, wg, p, wp)


def _rope_tables(positions):
    inv_freq = ROPE_THETA ** (-jnp.arange(0, MLA_ROPE, 2, dtype=F32) / MLA_ROPE)
    ang = positions.astype(F32)[:, None] * inv_freq
    return jnp.cos(ang).T, jnp.sin(ang).T


def kernel(x, p, positions, norm_mix, norm_mlp, norm_ple, mla_w_in, mla_q_norm, mla_kv_norm, mla_w_uq, mla_w_ukv, mla_q_gain, mla_k_gain, mla_w_o, sb_w_qkv, sb_w_o, mlp_w_up, mlp_w_down, ple_w_proj, ple_w_gate):
    assert x.shape[0] == 1, "one sequence per call"
    depth = p.shape[0]
    cosT, sinT = _rope_tables(positions[0])
    col = lambda v: v.astype(F32)[:, None]
    row = lambda v: v.astype(F32)[None, :]
    h = x[0]
    for i in range(depth):
        j = i // N_MIXERS
        if i % N_MIXERS == 0:
            qT, k, vT = _mla_proj(
                h, row(norm_mix[i]), mla_w_in[j].T.astype(BF16),
                col(mla_q_norm[j]), col(mla_kv_norm[j]),
                mla_w_uq[j].T.astype(BF16), mla_w_ukv[j].T.astype(BF16),
                col(mla_q_gain[j]), col(mla_k_gain[j]), cosT, sinT)
            o = _mla_attn(qT, k, vT)
            w_o = mla_w_o[j]
        else:
            q, kT, v = _sb_proj(h, row(norm_mix[i]), sb_w_qkv[j].astype(BF16))
            o = _sb_attn(q, kT, v)
            w_o = sb_w_o[j]
        h1, xn = _out_proj(o, h, w_o.astype(BF16), row(norm_mlp[i]))
        u = _mlp_up(xn, mlp_w_up[i])
        h2, hg, ss = _mlp_down(u, mlp_w_down[i].astype(BF16), h1, row(norm_ple[i]))
        h = _ple(h2, hg, ss, ple_w_gate[i].astype(BF16), p[i, 0], ple_w_proj[i].astype(BF16))
    return h[None]
```

```python
import functools

import jax
import jax.numpy as jnp
from jax import lax
from jax.experimental import pallas as pl
from jax.experimental.pallas import tpu as pltpu

F32 = jnp.float32
BF16 = jnp.bfloat16

CHUNK = 64
N_MIXERS = 2
MLA_HEADS = 8
MLA_Q_LORA = 1024
MLA_KV_LORA = 512
MLA_NOPE = 128
MLA_ROPE = 64
MLA_QK = MLA_NOPE + MLA_ROPE
MLA_QK_PAD = 256
MLA_V = 128
MLA_V_AUG = MLA_V + 16
ROPE_THETA = 10000.0
LOG2E = 1.4426950408889634
SB_HEADS = 4
SB_HEAD_DIM = 256
EPS = 1e-6

MIB = 1024 * 1024
V7X_VMEM_BYTES = 64 * MIB
LANES = 128


def _params(semantics, vmem_mib):
    assert vmem_mib * MIB < V7X_VMEM_BYTES
    return pltpu.CompilerParams(dimension_semantics=semantics, vmem_limit_bytes=vmem_mib * MIB)


def _resident(shape):
    return pl.BlockSpec(shape, lambda *_: (0,) * len(shape), pipeline_mode=pl.Buffered(1))


def _resident_layer(stack, layer):
    tail = stack.shape[1:]
    return pl.BlockSpec((None,) + tail, lambda *_: (layer,) + (0,) * len(tail),
                        pipeline_mode=pl.Buffered(1))


def _rms_rows(x, gain):
    return x * lax.rsqrt(jnp.mean(x * x, axis=-1, keepdims=True) + EPS) * gain


def _neg_abs(x):
    bits = lax.bitcast_convert_type(x, jnp.uint32) | jnp.uint32(0x80000000)
    return lax.bitcast_convert_type(bits, F32)


def _rms_cols(x, gain):
    return x * lax.rsqrt(jnp.mean(x * x, axis=0, keepdims=True) + EPS) * gain


def _mla_proj_kernel(h_ref, g_ref, w_in_ref, qn_ref, kvn_ref, w_uq_ref, w_ukv_ref,
                     qg_ref, kg_ref, cos_ref, sin_ref, qT_ref, k_ref, vT_ref):
    tm = h_ref.shape[0]
    tok = slice(0, tm)
    y = _rms_rows(h_ref[tok, :], g_ref[...]).astype(BF16)
    projT = lax.dot_general(w_in_ref[...], y, (((1,), (1,)), ((), ())),
                            preferred_element_type=F32)
    cq = _rms_cols(projT[:MLA_Q_LORA], qn_ref[...]).astype(BF16)
    ckv = _rms_cols(projT[MLA_Q_LORA:MLA_Q_LORA + MLA_KV_LORA], kvn_ref[...]).astype(BF16)
    kr = projT[MLA_Q_LORA + MLA_KV_LORA:]
    qT = jnp.dot(w_uq_ref[...], cq, preferred_element_type=F32)
    kvT = jnp.dot(w_ukv_ref[...], ckv, preferred_element_type=F32)
    cos = cos_ref[:, tok]
    sin = sin_ref[:, tok]
    qg = qg_ref[...]
    kg = kg_ref[...]
    kr_ss = jnp.sum(kr * kr, axis=0, keepdims=True)
    scale = MLA_QK ** -0.5 * LOG2E
    half = MLA_ROPE // 2
    pad = MLA_QK_PAD - MLA_QK
    aug = MLA_V_AUG - MLA_V
    ones_tile = (lax.broadcasted_iota(jnp.int32, (aug, tm), 0) == 0).astype(F32).astype(BF16)
    for hd in range(MLA_HEADS):
        qh = _rms_cols(qT[hd * MLA_QK:(hd + 1) * MLA_QK], qg) * scale
        x1 = qh[MLA_NOPE:MLA_NOPE + half]
        x2 = qh[MLA_NOPE + half:]
        qT_ref[hd, 0:MLA_NOPE, tok] = qh[:MLA_NOPE].astype(BF16)
        qT_ref[hd, MLA_NOPE:MLA_NOPE + half, tok] = (x1 * cos - x2 * sin).astype(BF16)
        qT_ref[hd, MLA_NOPE + half:MLA_QK, tok] = (x2 * cos + x1 * sin).astype(BF16)
        qT_ref[hd, MLA_QK:, tok] = jnp.zeros((pad, tm), BF16)

        base = hd * (MLA_NOPE + MLA_V)
        kn = kvT[base:base + MLA_NOPE]
        inv = lax.rsqrt((jnp.sum(kn * kn, axis=0, keepdims=True) + kr_ss) * (1.0 / MLA_QK) + EPS)
        kn = kn * inv * kg[:MLA_NOPE]
        krh = kr * inv * kg[MLA_NOPE:]
        k1 = krh[:half]
        k2 = krh[half:]
        kT = jnp.concatenate(
            [kn, k1 * cos - k2 * sin, k2 * cos + k1 * sin, jnp.zeros((pad, tm), F32)], axis=0)
        k_ref[hd, tok, :] = kT.T.astype(BF16)
        vT_ref[hd, :MLA_V, tok] = kvT[base + MLA_NOPE:base + MLA_NOPE + MLA_V].astype(BF16)
        vT_ref[hd, MLA_V:, tok] = ones_tile


def _mla_proj(h, g, layer, w_inT, qn, kvn, w_uqT, w_ukvT, qg, kg, cosT, sinT, tm=256):
    S, D = h.shape
    tm = min(tm, S)
    H = MLA_HEADS
    return pl.pallas_call(
        _mla_proj_kernel,
        grid=(S // tm,),
        in_specs=[
            pl.BlockSpec((tm, D), lambda i: (i, 0)),
            _resident((1, D)),
            _resident_layer(w_inT, layer),
            _resident(qn.shape),
            _resident(kvn.shape),
            _resident_layer(w_uqT, layer),
            _resident_layer(w_ukvT, layer),
            _resident(qg.shape),
            _resident(kg.shape),
            pl.BlockSpec((MLA_ROPE // 2, tm), lambda i: (0, i)),
            pl.BlockSpec((MLA_ROPE // 2, tm), lambda i: (0, i)),
        ],
        out_specs=[
            pl.BlockSpec((H, MLA_QK_PAD, tm), lambda i: (0, 0, i)),
            pl.BlockSpec((H, tm, MLA_QK_PAD), lambda i: (0, i, 0)),
            pl.BlockSpec((H, MLA_V_AUG, tm), lambda i: (0, 0, i)),
        ],
        out_shape=[
            jax.ShapeDtypeStruct((H, MLA_QK_PAD, S), BF16),
            jax.ShapeDtypeStruct((H, S, MLA_QK_PAD), BF16),
            jax.ShapeDtypeStruct((H, MLA_V_AUG, S), BF16),
        ],
        compiler_params=_params(("parallel",), 52),
        name="mla_proj",
    )(h, g, w_inT, qn, kvn, w_uqT, w_ukvT, qg, kg, cosT, sinT)


MLA_DEPTH = 4


def _mla_attn_kernel(qT_ref, k_ref, vT_ref, o_ref, s0_ref, s1_ref, s2_ref, s3_ref, p0_ref, p1_ref,
                     m_ref, alpha_ref, acc_ref, *, TK):
    i = pl.program_id(1)
    N = MLA_DEPTH
    TQ = N * TK
    s_refs = (s0_ref, s1_ref, s2_ref, s3_ref)
    p_refs = (p0_ref, p1_ref)

    def scores(block, u):
        start = pl.multiple_of(block * TK, TK)
        s_refs[u % N][...] = jnp.dot(k_ref[0, pl.ds(start, TK), :], qT_ref[0],
                                     preferred_element_type=F32)

    def pv(block, u):
        start = pl.multiple_of(block * TK, TK)
        prod = jnp.dot(vT_ref[0, :, pl.ds(start, TK)], p_refs[u % 2][...],
                       preferred_element_type=F32)
        acc_ref[...] = alpha_ref[u % N] * acc_ref[...] + prod

    def running_max(u, chunk_offset):
        s_ref = s_refs[u % N]
        if chunk_offset is not None:
            kc = lax.broadcasted_iota(jnp.int32, (TK, TQ), 0) // CHUNK + chunk_offset
            qc = lax.broadcasted_iota(jnp.int32, (TK, TQ), 1) // CHUNK
            s_ref[...] = jnp.where(kc <= qc, s_ref[...], -jnp.inf)
        m_prev = m_ref[(u - 1) % N]
        m_new = jnp.maximum(m_prev, jnp.max(s_ref[...], axis=0, keepdims=True))
        m_ref[u % N] = m_new
        alpha_ref[u % N] = jnp.exp2(m_prev - m_new)

    def probabilities(u):
        p_refs[u % 2][...] = jnp.exp2(s_refs[u % N][...] - m_ref[u % N]).astype(BF16)

    def step(u, block_ahead, block_behind, offset_next):
        scores(block_ahead, u + 2)
        pv(block_behind, u - 1)
        probabilities(u)
        running_max(u + 1, offset_next)

    first = N * i
    per_block = TK // CHUNK
    acc_ref[...] = jnp.zeros(acc_ref.shape, F32)
    p_refs[1][...] = jnp.zeros(p_refs[1].shape, BF16)
    alpha_ref[N - 1] = jnp.ones(alpha_ref.shape[1:], F32)
    m_ref[N - 1] = jnp.full(m_ref.shape[1:], -jnp.inf, F32)
    scores(first, 0)
    scores(first + 1, 1)
    running_max(0, 0)

    step(0, first + 2, first, per_block)
    step(1, first + 3, first, 2 * per_block)
    step(2, 0, first + 1, 3 * per_block)
    step(3, 1, first + 2, None)

    def body(j, carry):
        base = N * j
        step(4, base + 2, jnp.where(j == 0, first + 3, base - 1), None)
        step(5, base + 3, base, None)
        step(6, base + 4, base + 1, None)
        step(7, base + 5, base + 2, None)
        return carry

    lax.fori_loop(0, i, body, 0)
    pv(jnp.where(i == 0, first + 3, first - 1), N - 1)
    o = acc_ref[:MLA_V, :] / acc_ref[MLA_V:MLA_V + 1, :]
    o_ref[...] = o.T.astype(BF16)


def _mla_attn(qT, k, vT, tk=512):
    H, _, S = qT.shape
    tk = min(tk, S // MLA_DEPTH)
    tq = MLA_DEPTH * tk
    return pl.pallas_call(
        functools.partial(_mla_attn_kernel, TK=tk),
        grid=(H, S // tq),
        in_specs=[
            pl.BlockSpec((1, MLA_QK_PAD, tq), lambda h, i: (h, 0, i)),
            pl.BlockSpec((1, S, MLA_QK_PAD), lambda h, i: (h, 0, 0)),
            pl.BlockSpec((1, MLA_V_AUG, S), lambda h, i: (h, 0, 0)),
        ],
        out_specs=pl.BlockSpec((tq, MLA_V), lambda h, i: (i, h)),
        out_shape=jax.ShapeDtypeStruct((S, H * MLA_V), BF16),
        scratch_shapes=(
            [pltpu.VMEM((tk, tq), F32)] * MLA_DEPTH
            + [pltpu.VMEM((tk, tq), BF16)] * 2
            + [pltpu.VMEM((MLA_DEPTH, 1, tq), F32),
               pltpu.VMEM((MLA_DEPTH, 1, tq), F32),
               pltpu.VMEM((MLA_V_AUG, tq), F32)]
        ),
        compiler_params=_params(("parallel", "arbitrary"), 54),
        name="mla_attn",
    )(qT, k, vT)


def _sb_proj_kernel(h_ref, g_ref, w_ref, q_ref, kT_ref, v_ref):
    HD = SB_HEADS * SB_HEAD_DIM
    y = _rms_rows(h_ref[...], g_ref[...]).astype(BF16)
    qkv = jnp.dot(y, w_ref[...], preferred_element_type=F32)
    q_ref[...] = (qkv[:, :HD] * (SB_HEAD_DIM ** -0.5 * LOG2E)).astype(BF16)
    v_ref[...] = qkv[:, 2 * HD:].astype(BF16)
    for hd in range(SB_HEADS):
        k = qkv[:, HD + hd * SB_HEAD_DIM:HD + (hd + 1) * SB_HEAD_DIM]
        kT_ref[hd] = k.T.astype(BF16)


def _sb_proj(h, g, layer, w, tm=256):
    S, D = h.shape
    tm = min(tm, S)
    HD = SB_HEADS * SB_HEAD_DIM
    return pl.pallas_call(
        _sb_proj_kernel,
        grid=(S // tm,),
        in_specs=[
            pl.BlockSpec((tm, D), lambda i: (i, 0)),
            _resident((1, D)),
            _resident_layer(w, layer),
        ],
        out_specs=[
            pl.BlockSpec((tm, HD), lambda i: (i, 0)),
            pl.BlockSpec((SB_HEADS, SB_HEAD_DIM, tm), lambda i: (0, 0, i)),
            pl.BlockSpec((tm, HD), lambda i: (i, 0)),
        ],
        out_shape=[
            jax.ShapeDtypeStruct((S, HD), BF16),
            jax.ShapeDtypeStruct((SB_HEADS, SB_HEAD_DIM, S), BF16),
            jax.ShapeDtypeStruct((S, HD), BF16),
        ],
        compiler_params=_params(("parallel",), 52),
        name="sb_proj",
    )(h, g, w)


def _sb_attn_kernel(q_ref, kT_ref, v_ref, o_ref, za_ref, zb_ref, aa_ref, ab_ref,
                    acc_ref, after_ref, later_ref, *, TK, R):
    i = pl.program_id(1)
    row = lax.broadcasted_iota(jnp.int32, (TK, TK), 0)
    col = lax.broadcasted_iota(jnp.int32, (TK, TK), 1)
    later_ref[...] = (row > col).astype(F32).astype(BF16)
    last = R * i + R - 1

    def key_start(t):
        return pl.multiple_of(jnp.maximum(last - t, 0) * TK, TK)

    def logits(t, z_ref):
        z_ref[...] = jnp.dot(q_ref[...], kT_ref[0, :, pl.ds(key_start(t), TK)],
                             preferred_element_type=F32)

    def pending_av(t, a_ref):
        acc_ref[...] += jnp.dot(a_ref[...], v_ref[pl.ds(key_start(jnp.maximum(t, 0)), TK), :],
                                preferred_element_type=F32)

    def weights(z_ref, a_ref, mask):
        for r in range(R):
            rows = slice(r * TK, (r + 1) * TK)
            kind = None if mask is None else mask[r]
            if kind == "none":
                a_ref[rows, :] = jnp.zeros((TK, TK), BF16)
                continue
            z = z_ref[rows, :]
            sp = jnp.maximum(z, 0.0) + jnp.log2(1.0 + jnp.exp2(_neg_abs(z)))
            if kind == "diag":
                valid = col < row
                sp = jnp.where(valid, sp, 0.0)
            within = jnp.dot(sp.astype(BF16), later_ref[...], preferred_element_type=F32)
            after = after_ref[rows, :]
            a = jnp.exp2(z - sp - within - after)
            if kind == "diag":
                a = jnp.where(valid, a, 0.0)
            a_ref[rows, :] = a.astype(BF16)
            after_ref[rows, :] = after + jnp.sum(sp, axis=1, keepdims=True)

    def pair(t, masks):
        logits(t + 1, zb_ref)
        pending_av(t - 1, ab_ref)
        weights(za_ref, aa_ref, masks[0])
        logits(t + 2, za_ref)
        pending_av(t, aa_ref)
        weights(zb_ref, ab_ref, masks[1])

    acc_ref[...] = jnp.zeros(acc_ref.shape, F32)
    after_ref[...] = jnp.zeros(after_ref.shape, F32)
    ab_ref[...] = jnp.zeros(ab_ref.shape, BF16)
    logits(0, za_ref)

    def diagonal_mask(t):
        beside = R - 1 - t
        return tuple("diag" if r == beside else ("none" if r < beside else None) for r in range(R))

    for t in range(0, R, 2):
        pair(t, (diagonal_mask(t), diagonal_mask(t + 1)))

    def body(j, carry):
        pair(2 * j + R, (None, None))
        return carry

    lax.fori_loop(0, (R // 2) * i, body, 0)
    pending_av(last, ab_ref)
    o_ref[...] = acc_ref[...].astype(BF16)


def _sb_attn(q, kT, v, tk=256, r=4):
    S = q.shape[0]
    tk = min(tk, S // r)
    tq = r * tk
    Dh = SB_HEAD_DIM
    return pl.pallas_call(
        functools.partial(_sb_attn_kernel, TK=tk, R=r),
        grid=(SB_HEADS, S // tq),
        in_specs=[
            pl.BlockSpec((tq, Dh), lambda h, i: (i, h)),
            pl.BlockSpec((1, Dh, S), lambda h, i: (h, 0, 0)),
            pl.BlockSpec((S, Dh), lambda h, i: (0, h)),
        ],
        out_specs=pl.BlockSpec((tq, Dh), lambda h, i: (i, h)),
        out_shape=jax.ShapeDtypeStruct((S, SB_HEADS * Dh), BF16),
        scratch_shapes=[
            pltpu.VMEM((tq, tk), F32),
            pltpu.VMEM((tq, tk), F32),
            pltpu.VMEM((tq, tk), BF16),
            pltpu.VMEM((tq, tk), BF16),
            pltpu.VMEM((tq, Dh), F32),
            pltpu.VMEM((tq, 1), F32),
            pltpu.VMEM((tk, tk), BF16),
        ],
        compiler_params=_params(("parallel", "arbitrary"), 48),
        name="sb_attn",
    )(q, kT, v)


def _out_proj_kernel(o_ref, h_ref, w_ref, g_ref, h1_ref, xn_ref):
    h1 = h_ref[...] + jnp.dot(o_ref[...], w_ref[...], preferred_element_type=F32)
    h1_ref[...] = h1
    xn_ref[...] = _rms_rows(h1, g_ref[...]).astype(BF16)


def _out_proj(o, h, layer, w, g, tm=256):
    S, D = h.shape
    tm = min(tm, S)
    return pl.pallas_call(
        _out_proj_kernel,
        grid=(S // tm,),
        in_specs=[
            pl.BlockSpec((tm, o.shape[1]), lambda i: (i, 0)),
            pl.BlockSpec((tm, D), lambda i: (i, 0)),
            _resident_layer(w, layer),
            _resident((1, D)),
        ],
        out_specs=[
            pl.BlockSpec((tm, D), lambda i: (i, 0)),
            pl.BlockSpec((tm, D), lambda i: (i, 0)),
        ],
        out_shape=[
            jax.ShapeDtypeStruct((S, D), F32),
            jax.ShapeDtypeStruct((S, D), BF16),
        ],
        compiler_params=_params(("parallel",), 48),
        name="out_proj",
    )(o, h, w, g)


def _mlp_up_kernel(x_ref, w_ref, u_ref):
    u = jnp.dot(x_ref[...], w_ref[...].astype(BF16), preferred_element_type=F32)
    u = jnp.maximum(u, 0.0)
    u_ref[...] = (u * u).astype(BF16)


def _mlp_up(xn, layer, w, tm=2048, tn=512):
    S, D = xn.shape
    F = w.shape[2]
    tm, tn = min(tm, S), min(tn, F)
    return pl.pallas_call(
        _mlp_up_kernel,
        grid=(S // tm, F // tn),
        in_specs=[
            pl.BlockSpec((tm, D), lambda i, j: (i, 0), pipeline_mode=pl.Buffered(1)),
            pl.BlockSpec((None, D, tn), lambda i, j: (layer, 0, j)),
        ],
        out_specs=pl.BlockSpec((tm, tn), lambda i, j: (i, j)),
        out_shape=jax.ShapeDtypeStruct((S, F), BF16),
        compiler_params=_params(("parallel", "arbitrary"), 52),
        name="mlp_up",
    )(xn, w)


def _mlp_down_kernel(u_ref, w_ref, r_ref, g_ref, o_ref, hg_ref, ss_ref):
    k = pl.program_id(2)

    @pl.when(k == 0)
    def _():
        o_ref[...] = r_ref[...]

    o_ref[...] += jnp.dot(u_ref[...], w_ref[...], preferred_element_type=F32)

    @pl.when(k == pl.num_programs(2) - 1)
    def _():
        o = o_ref[...]
        hg_ref[...] = (o * g_ref[...]).astype(BF16)
        ss_ref[...] = jnp.broadcast_to(jnp.sum(o * o, axis=-1, keepdims=True), ss_ref.shape)


def _mlp_down(u, layer, w, res, g, tm=1024, tn=1024, tk=2048):
    S, F = u.shape
    D = w.shape[2]
    tm, tn, tk = min(tm, S), min(tn, D), min(tk, F)
    return pl.pallas_call(
        _mlp_down_kernel,
        grid=(S // tm, D // tn, F // tk),
        in_specs=[
            pl.BlockSpec((tm, tk), lambda i, j, k: (i, k)),
            pl.BlockSpec((None, tk, tn), lambda i, j, k: (layer, k, j)),
            pl.BlockSpec((tm, tn), lambda i, j, k: (i, j)),
            pl.BlockSpec((1, tn), lambda i, j, k: (0, j)),
        ],
        out_specs=[
            pl.BlockSpec((tm, tn), lambda i, j, k: (i, j)),
            pl.BlockSpec((tm, tn), lambda i, j, k: (i, j)),
            pl.BlockSpec((tm, LANES), lambda i, j, k: (i, j)),
        ],
        out_shape=[
            jax.ShapeDtypeStruct((S, D), F32),
            jax.ShapeDtypeStruct((S, D), BF16),
            jax.ShapeDtypeStruct((S, (D // tn) * LANES), F32),
        ],
        compiler_params=_params(("parallel", "parallel", "arbitrary"), 48),
        name="mlp_down",
    )(u, w, res, g)


def _ple_kernel(hg_ref, ss_ref, h_ref, wg_ref, p_ref, wp_ref, o_ref, *, d_model):
    ss = ss_ref[...]
    total = ss[:, 0:1]
    for t in range(1, ss.shape[1] // LANES):
        total = total + ss[:, t * LANES:t * LANES + 1]
    inv = lax.rsqrt(total * (1.0 / d_model) + EPS)
    logits = jnp.dot(hg_ref[...], wg_ref[...], preferred_element_type=F32) * inv
    gate = 1.0 / (1.0 + jnp.exp(-logits))
    emb = jnp.dot(p_ref[...].astype(BF16), wp_ref[...], preferred_element_type=F32)
    o_ref[...] = h_ref[...] + emb * gate


def _ple(h, hg, ss, layer, wg, p, wp, tm=1024, tn=512):
    S, D = h.shape
    P = p.shape[-1]
    tm, tn = min(tm, S), min(tn, D)
    return pl.pallas_call(
        functools.partial(_ple_kernel, d_model=D),
        grid=(S // tm, D // tn),
        in_specs=[
            pl.BlockSpec((tm, D), lambda i, j: (i, 0)),
            pl.BlockSpec((tm, ss.shape[1]), lambda i, j: (i, 0)),
            pl.BlockSpec((tm, tn), lambda i, j: (i, j)),
            pl.BlockSpec((None, D, tn), lambda i, j: (layer, 0, j)),
            pl.BlockSpec((None, None, tm, P), lambda i, j: (layer, 0, i, 0)),
            pl.BlockSpec((None, P, tn), lambda i, j: (layer, 0, j)),
        ],
        out_specs=pl.BlockSpec((tm, tn), lambda i, j: (i, j)),
        out_shape=jax.ShapeDtypeStruct((S, D), F32),
        compiler_params=_params(("parallel", "arbitrary"), 48),
        name="ple_gate",
    )(hg, ss, h, wg, p, wp)


def _rope_tables(positions):
    inv_freq = ROPE_THETA ** (-jnp.arange(0, MLA_ROPE, 2, dtype=F32) / MLA_ROPE)
    ang = positions.astype(F32)[:, None] * inv_freq
    return jnp.cos(ang).T, jnp.sin(ang).T


def kernel(x, p, positions, norm_mix, norm_mlp, norm_ple, mla_w_in, mla_q_norm, mla_kv_norm, mla_w_uq, mla_w_ukv, mla_q_gain, mla_k_gain, mla_w_o, sb_w_qkv, sb_w_o, mlp_w_up, mlp_w_down, ple_w_proj, ple_w_gate):
    assert x.shape[0] == 1, "one sequence per call"
    depth = p.shape[0]
    cosT, sinT = _rope_tables(positions[0])
    col = lambda v: v.astype(F32)[:, None]
    row = lambda v: v.astype(F32)[None, :]
    transposed = lambda w: jnp.swapaxes(w, 1, 2).astype(BF16)
    mla_in_t, mla_uq_t, mla_ukv_t = transposed(mla_w_in), transposed(mla_w_uq), transposed(mla_w_ukv)
    mla_o, sb_qkv, sb_o = mla_w_o.astype(BF16), sb_w_qkv.astype(BF16), sb_w_o.astype(BF16)
    w_down, w_gate, w_proj = mlp_w_down.astype(BF16), ple_w_gate.astype(BF16), ple_w_proj.astype(BF16)
    h = x[0]
    for i in range(depth):
        j = i // N_MIXERS
        if i % N_MIXERS == 0:
            qT, k, vT = _mla_proj(
                h, row(norm_mix[i]), j, mla_in_t, col(mla_q_norm[j]), col(mla_kv_norm[j]),
                mla_uq_t, mla_ukv_t, col(mla_q_gain[j]), col(mla_k_gain[j]), cosT, sinT)
            o = _mla_attn(qT, k, vT)
            w_o = mla_o
        else:
            q, kT, v = _sb_proj(h, row(norm_mix[i]), j, sb_qkv)
            o = _sb_attn(q, kT, v)
            w_o = sb_o
        h1, xn = _out_proj(o, h, j, w_o, row(norm_mlp[i]))
        u = _mlp_up(xn, i, mlp_w_up)
        h2, hg, ss = _mlp_down(u, i, w_down, h1, row(norm_ple[i]))
        h = _ple(h2, hg, ss, i, w_gate, p, w_proj)
    return h[None]
```

```python
import functools

import jax
import jax.numpy as jnp
from jax import lax
from jax.experimental import pallas as pl
from jax.experimental.pallas import tpu as pltpu

F32 = jnp.float32
BF16 = jnp.bfloat16

CHUNK = 64
N_MIXERS = 2
MLA_HEADS = 8
MLA_Q_LORA = 1024
MLA_KV_LORA = 512
MLA_NOPE = 128
MLA_ROPE = 64
MLA_QK = MLA_NOPE + MLA_ROPE
MLA_QK_PAD = 256
MLA_V = 128
MLA_V_AUG = MLA_V + 16
ROPE_THETA = 10000.0
LOG2E = 1.4426950408889634
SB_HEADS = 4
SB_HEAD_DIM = 256
EPS = 1e-6

MIB = 1024 * 1024
V7X_VMEM_BYTES = 64 * MIB
LANES = 128


def _params(semantics, vmem_mib):
    assert vmem_mib * MIB < V7X_VMEM_BYTES
    return pltpu.CompilerParams(dimension_semantics=semantics, vmem_limit_bytes=vmem_mib * MIB)


def _resident(shape):
    return pl.BlockSpec(shape, lambda *_: (0,) * len(shape), pipeline_mode=pl.Buffered(1))


def _resident_layer(stack, layer):
    tail = stack.shape[1:]
    return pl.BlockSpec((None,) + tail, lambda *_: (layer,) + (0,) * len(tail),
                        pipeline_mode=pl.Buffered(1))


def _rms_rows(x, gain):
    return x * lax.rsqrt(jnp.mean(x * x, axis=-1, keepdims=True) + EPS) * gain


def _neg_abs(x):
    bits = lax.bitcast_convert_type(x, jnp.uint32) | jnp.uint32(0x80000000)
    return lax.bitcast_convert_type(bits, F32)


def _rms_cols(x, gain):
    return x * lax.rsqrt(jnp.mean(x * x, axis=0, keepdims=True) + EPS) * gain


def _mla_proj_kernel(h_ref, g_ref, w_in_ref, qn_ref, kvn_ref, w_uq_ref, w_ukv_ref,
                     qg_ref, kg_ref, cos_ref, sin_ref, qT_ref, k_ref, vT_ref,
                     cq_ref, ckv_ref, kr_ref):
    tm = h_ref.shape[0]
    tok = slice(0, tm)

    @pl.when(pl.program_id(0) == 0)
    def _():
        cq_ref[...] = jnp.zeros(cq_ref.shape, BF16)
        ckv_ref[...] = jnp.zeros(ckv_ref.shape, BF16)
        kr_ref[...] = jnp.zeros(kr_ref.shape, F32)

    _mla_heads(tok, cq_ref, ckv_ref, kr_ref, w_uq_ref, w_ukv_ref, qg_ref, kg_ref, cos_ref, sin_ref,
               qT_ref, k_ref, vT_ref)

    y = _rms_rows(h_ref[tok, :], g_ref[...]).astype(BF16)
    projT = lax.dot_general(w_in_ref[...], y, (((1,), (1,)), ((), ())),
                            preferred_element_type=F32)
    cq_ref[...] = _rms_cols(projT[:MLA_Q_LORA], qn_ref[...]).astype(BF16)
    ckv_ref[...] = _rms_cols(projT[MLA_Q_LORA:MLA_Q_LORA + MLA_KV_LORA], kvn_ref[...]).astype(BF16)
    kr_ref[...] = projT[MLA_Q_LORA + MLA_KV_LORA:]


def _mla_heads(tok, cq_ref, ckv_ref, kr_ref, w_uq_ref, w_ukv_ref, qg_ref, kg_ref, cos_ref, sin_ref,
               qT_ref, k_ref, vT_ref):
    tm = tok.stop - tok.start
    kr = kr_ref[...]
    qT = jnp.dot(w_uq_ref[...], cq_ref[...], preferred_element_type=F32)
    kvT = jnp.dot(w_ukv_ref[...], ckv_ref[...], preferred_element_type=F32)
    cos = cos_ref[:, tok]
    sin = sin_ref[:, tok]
    qg = qg_ref[...]
    kg = kg_ref[...]
    kr_ss = jnp.sum(kr * kr, axis=0, keepdims=True)
    scale = MLA_QK ** -0.5 * LOG2E
    half = MLA_ROPE // 2
    pad = MLA_QK_PAD - MLA_QK
    aug = MLA_V_AUG - MLA_V
    ones_tile = (lax.broadcasted_iota(jnp.int32, (aug, tm), 0) == 0).astype(F32).astype(BF16)
    for hd in range(MLA_HEADS):
        qh = _rms_cols(qT[hd * MLA_QK:(hd + 1) * MLA_QK], qg) * scale
        x1 = qh[MLA_NOPE:MLA_NOPE + half]
        x2 = qh[MLA_NOPE + half:]
        qT_ref[hd, 0:MLA_NOPE, tok] = qh[:MLA_NOPE].astype(BF16)
        qT_ref[hd, MLA_NOPE:MLA_NOPE + half, tok] = (x1 * cos - x2 * sin).astype(BF16)
        qT_ref[hd, MLA_NOPE + half:MLA_QK, tok] = (x2 * cos + x1 * sin).astype(BF16)
        qT_ref[hd, MLA_QK:, tok] = jnp.zeros((pad, tm), BF16)

        base = hd * (MLA_NOPE + MLA_V)
        kn = kvT[base:base + MLA_NOPE]
        inv = lax.rsqrt((jnp.sum(kn * kn, axis=0, keepdims=True) + kr_ss) * (1.0 / MLA_QK) + EPS)
        kn = kn * inv * kg[:MLA_NOPE]
        krh = kr * inv * kg[MLA_NOPE:]
        k1 = krh[:half]
        k2 = krh[half:]
        kT = jnp.concatenate(
            [kn, k1 * cos - k2 * sin, k2 * cos + k1 * sin, jnp.zeros((pad, tm), F32)], axis=0)
        k_ref[hd, tok, :] = kT.T.astype(BF16)
        vT_ref[hd, :MLA_V, tok] = kvT[base + MLA_NOPE:base + MLA_NOPE + MLA_V].astype(BF16)
        vT_ref[hd, MLA_V:, tok] = ones_tile


def _mla_proj(h, g, layer, w_inT, qn, kvn, w_uqT, w_ukvT, qg, kg, cosT, sinT, tm=256):
    S, D = h.shape
    tm = min(tm, S)
    H = MLA_HEADS
    n = S // tm
    latents = lambda s: jnp.minimum(s, n - 1)
    heads = lambda s: jnp.maximum(s - 1, 0)
    return pl.pallas_call(
        _mla_proj_kernel,
        grid=(n + 1,),
        in_specs=[
            pl.BlockSpec((tm, D), lambda s: (latents(s), 0)),
            _resident((1, D)),
            _resident_layer(w_inT, layer),
            _resident(qn.shape),
            _resident(kvn.shape),
            _resident_layer(w_uqT, layer),
            _resident_layer(w_ukvT, layer),
            _resident(qg.shape),
            _resident(kg.shape),
            pl.BlockSpec((MLA_ROPE // 2, tm), lambda s: (0, heads(s))),
            pl.BlockSpec((MLA_ROPE // 2, tm), lambda s: (0, heads(s))),
        ],
        out_specs=[
            pl.BlockSpec((H, MLA_QK_PAD, tm), lambda s: (0, 0, heads(s))),
            pl.BlockSpec((H, tm, MLA_QK_PAD), lambda s: (0, heads(s), 0)),
            pl.BlockSpec((H, MLA_V_AUG, tm), lambda s: (0, 0, heads(s))),
        ],
        out_shape=[
            jax.ShapeDtypeStruct((H, MLA_QK_PAD, S), BF16),
            jax.ShapeDtypeStruct((H, S, MLA_QK_PAD), BF16),
            jax.ShapeDtypeStruct((H, MLA_V_AUG, S), BF16),
        ],
        scratch_shapes=[
            pltpu.VMEM((MLA_Q_LORA, tm), BF16),
            pltpu.VMEM((MLA_KV_LORA, tm), BF16),
            pltpu.VMEM((MLA_ROPE, tm), F32),
        ],
        compiler_params=_params(("arbitrary",), 52),
        name="mla_proj",
    )(h, g, w_inT, qn, kvn, w_uqT, w_ukvT, qg, kg, cosT, sinT)


MLA_DEPTH = 4


def _mla_attn_kernel(qT_ref, k_ref, vT_ref, o_ref, s0_ref, s1_ref, s2_ref, s3_ref, p0_ref, p1_ref,
                     m_ref, alpha_ref, acc_ref, *, TK):
    i = pl.program_id(1)
    N = MLA_DEPTH
    TQ = N * TK
    s_refs = (s0_ref, s1_ref, s2_ref, s3_ref)
    p_refs = (p0_ref, p1_ref)

    def lo(u):
        return u * TK if 0 <= u < N else 0

    def scores(block, u):
        start = pl.multiple_of(block * TK, TK)
        s_refs[u % N][:, lo(u):] = jnp.dot(k_ref[0, pl.ds(start, TK), :], qT_ref[0, :, lo(u):],
                                           preferred_element_type=F32)

    def pv(block, u, trimmed=True):
        c = lo(u) if trimmed else 0
        start = pl.multiple_of(block * TK, TK)
        prod = jnp.dot(vT_ref[0, :, pl.ds(start, TK)], p_refs[u % 2][:, c:],
                       preferred_element_type=F32)
        acc_ref[:, c:] = alpha_ref[u % N, :, c:] * acc_ref[:, c:] + prod

    def running_max(u, diagonal):
        s_ref = s_refs[u % N]
        c = lo(u)
        if diagonal:
            kc = lax.broadcasted_iota(jnp.int32, (TK, TK), 0) // CHUNK
            qc = lax.broadcasted_iota(jnp.int32, (TK, TK), 1) // CHUNK
            s_ref[:, c:c + TK] = jnp.where(kc <= qc, s_ref[:, c:c + TK], -jnp.inf)
        m_prev = m_ref[(u - 1) % N, :, c:]
        m_new = jnp.maximum(m_prev, jnp.max(s_ref[:, c:], axis=0, keepdims=True))
        m_ref[u % N, :, c:] = m_new
        alpha_ref[u % N, :, c:] = jnp.exp2(m_prev - m_new)
        if c:
            m_ref[u % N, :, :c] = m_ref[(u - 1) % N, :, :c]
            alpha_ref[u % N, :, :c] = jnp.ones((1, c), F32)

    def probabilities(u, zero_hidden=False):
        c = lo(u)
        p_refs[u % 2][:, c:] = jnp.exp2(s_refs[u % N][:, c:] - m_ref[u % N, :, c:]).astype(BF16)
        if zero_hidden and c:
            p_refs[u % 2][:, :c] = jnp.zeros((TK, c), BF16)

    def step(u, block_ahead, block_behind, diagonal_next=False, last_diagonal=False):
        scores(block_ahead, u + 2)
        pv(block_behind, u - 1, trimmed=not (u - 1 == N - 1))
        probabilities(u, zero_hidden=last_diagonal)
        running_max(u + 1, diagonal_next)

    first = N * i
    acc_ref[...] = jnp.zeros(acc_ref.shape, F32)
    p_refs[1][...] = jnp.zeros(p_refs[1].shape, BF16)
    alpha_ref[N - 1] = jnp.ones(alpha_ref.shape[1:], F32)
    m_ref[N - 1] = jnp.full(m_ref.shape[1:], -jnp.inf, F32)
    scores(first, 0)
    scores(first + 1, 1)
    running_max(0, True)

    step(0, first + 2, first, diagonal_next=True)
    step(1, first + 3, first, diagonal_next=True)
    step(2, 0, first + 1, diagonal_next=True)
    step(3, 1, first + 2, last_diagonal=True)

    def body(j, carry):
        base = N * j
        step(N, base + 2, jnp.where(j == 0, first + 3, base - 1))
        step(N + 1, base + 3, base)
        step(N + 2, base + 4, base + 1)
        step(N + 3, base + 5, base + 2)
        return carry

    lax.fori_loop(0, i, body, 0)
    pv(jnp.where(i == 0, first + 3, first - 1), N - 1, trimmed=False)
    o = acc_ref[:MLA_V, :] / acc_ref[MLA_V:MLA_V + 1, :]
    o_ref[...] = o.T.astype(BF16)


def _mla_attn(qT, k, vT, tk=512):
    H, _, S = qT.shape
    tk = min(tk, S // MLA_DEPTH)
    tq = MLA_DEPTH * tk
    return pl.pallas_call(
        functools.partial(_mla_attn_kernel, TK=tk),
        grid=(H, S // tq),
        in_specs=[
            pl.BlockSpec((1, MLA_QK_PAD, tq), lambda h, i: (h, 0, i)),
            pl.BlockSpec((1, S, MLA_QK_PAD), lambda h, i: (h, 0, 0)),
            pl.BlockSpec((1, MLA_V_AUG, S), lambda h, i: (h, 0, 0)),
        ],
        out_specs=pl.BlockSpec((tq, MLA_V), lambda h, i: (i, h)),
        out_shape=jax.ShapeDtypeStruct((S, H * MLA_V), BF16),
        scratch_shapes=(
            [pltpu.VMEM((tk, tq), F32)] * MLA_DEPTH
            + [pltpu.VMEM((tk, tq), BF16)] * 2
            + [pltpu.VMEM((MLA_DEPTH, 1, tq), F32),
               pltpu.VMEM((MLA_DEPTH, 1, tq), F32),
               pltpu.VMEM((MLA_V_AUG, tq), F32)]
        ),
        compiler_params=_params(("parallel", "arbitrary"), 54),
        name="mla_attn",
    )(qT, k, vT)


def _sb_proj_kernel(h_ref, g_ref, w_ref, q_ref, kT_ref, v_ref):
    HD = SB_HEADS * SB_HEAD_DIM
    y = _rms_rows(h_ref[...], g_ref[...]).astype(BF16)
    qkv = jnp.dot(y, w_ref[...], preferred_element_type=F32)
    q_ref[...] = (qkv[:, :HD] * (SB_HEAD_DIM ** -0.5 * LOG2E)).astype(BF16)
    v_ref[...] = qkv[:, 2 * HD:].astype(BF16)
    for hd in range(SB_HEADS):
        k = qkv[:, HD + hd * SB_HEAD_DIM:HD + (hd + 1) * SB_HEAD_DIM]
        kT_ref[hd] = k.T.astype(BF16)


def _sb_proj(h, g, layer, w, tm=256):
    S, D = h.shape
    tm = min(tm, S)
    HD = SB_HEADS * SB_HEAD_DIM
    return pl.pallas_call(
        _sb_proj_kernel,
        grid=(S // tm,),
        in_specs=[
            pl.BlockSpec((tm, D), lambda i: (i, 0)),
            _resident((1, D)),
            _resident_layer(w, layer),
        ],
        out_specs=[
            pl.BlockSpec((tm, HD), lambda i: (i, 0)),
            pl.BlockSpec((SB_HEADS, SB_HEAD_DIM, tm), lambda i: (0, 0, i)),
            pl.BlockSpec((tm, HD), lambda i: (i, 0)),
        ],
        out_shape=[
            jax.ShapeDtypeStruct((S, HD), BF16),
            jax.ShapeDtypeStruct((SB_HEADS, SB_HEAD_DIM, S), BF16),
            jax.ShapeDtypeStruct((S, HD), BF16),
        ],
        compiler_params=_params(("parallel",), 52),
        name="sb_proj",
    )(h, g, w)


def _sb_attn_kernel(q_ref, kT_ref, v_ref, o_ref, za_ref, zb_ref, aa_ref, ab_ref,
                    acc_ref, after_ref, later_ref, *, TK, R):
    i = pl.program_id(1)
    row = lax.broadcasted_iota(jnp.int32, (TK, TK), 0)
    col = lax.broadcasted_iota(jnp.int32, (TK, TK), 1)
    later_ref[...] = (row > col).astype(F32).astype(BF16)
    last = R * i + R - 1

    def key_start(t):
        return pl.multiple_of(jnp.maximum(last - t, 0) * TK, TK)

    def logits(t, z_ref):
        z_ref[...] = jnp.dot(q_ref[...], kT_ref[0, :, pl.ds(key_start(t), TK)],
                             preferred_element_type=F32)

    def pending_av(t, a_ref):
        acc_ref[...] += jnp.dot(a_ref[...], v_ref[pl.ds(key_start(jnp.maximum(t, 0)), TK), :],
                                preferred_element_type=F32)

    def weights(z_ref, a_ref, mask):
        for r in range(R):
            rows = slice(r * TK, (r + 1) * TK)
            kind = None if mask is None else mask[r]
            if kind == "none":
                a_ref[rows, :] = jnp.zeros((TK, TK), BF16)
                continue
            z = z_ref[rows, :]
            sp = jnp.maximum(z, 0.0) + jnp.log2(1.0 + jnp.exp2(_neg_abs(z)))
            if kind == "diag":
                valid = col < row
                sp = jnp.where(valid, sp, 0.0)
            within = jnp.dot(sp.astype(BF16), later_ref[...], preferred_element_type=F32)
            after = after_ref[rows, :]
            a = jnp.exp2(z - sp - within - after)
            if kind == "diag":
                a = jnp.where(valid, a, 0.0)
            a_ref[rows, :] = a.astype(BF16)
            after_ref[rows, :] = after + jnp.sum(sp, axis=1, keepdims=True)

    def pair(t, masks):
        logits(t + 1, zb_ref)
        pending_av(t - 1, ab_ref)
        weights(za_ref, aa_ref, masks[0])
        logits(t + 2, za_ref)
        pending_av(t, aa_ref)
        weights(zb_ref, ab_ref, masks[1])

    acc_ref[...] = jnp.zeros(acc_ref.shape, F32)
    after_ref[...] = jnp.zeros(after_ref.shape, F32)
    ab_ref[...] = jnp.zeros(ab_ref.shape, BF16)
    logits(0, za_ref)

    def diagonal_mask(t):
        beside = R - 1 - t
        return tuple("diag" if r == beside else ("none" if r < beside else None) for r in range(R))

    for t in range(0, R, 2):
        pair(t, (diagonal_mask(t), diagonal_mask(t + 1)))

    def body(j, carry):
        pair(2 * j + R, (None, None))
        return carry

    lax.fori_loop(0, (R // 2) * i, body, 0)
    pending_av(last, ab_ref)
    o_ref[...] = acc_ref[...].astype(BF16)


def _sb_attn(q, kT, v, tk=256, r=4):
    S = q.shape[0]
    tk = min(tk, S // r)
    tq = r * tk
    Dh = SB_HEAD_DIM
    return pl.pallas_call(
        functools.partial(_sb_attn_kernel, TK=tk, R=r),
        grid=(SB_HEADS, S // tq),
        in_specs=[
            pl.BlockSpec((tq, Dh), lambda h, i: (i, h)),
            pl.BlockSpec((1, Dh, S), lambda h, i: (h, 0, 0)),
            pl.BlockSpec((S, Dh), lambda h, i: (0, h)),
        ],
        out_specs=pl.BlockSpec((tq, Dh), lambda h, i: (i, h)),
        out_shape=jax.ShapeDtypeStruct((S, SB_HEADS * Dh), BF16),
        scratch_shapes=[
            pltpu.VMEM((tq, tk), F32),
            pltpu.VMEM((tq, tk), F32),
            pltpu.VMEM((tq, tk), BF16),
            pltpu.VMEM((tq, tk), BF16),
            pltpu.VMEM((tq, Dh), F32),
            pltpu.VMEM((tq, 1), F32),
            pltpu.VMEM((tk, tk), BF16),
        ],
        compiler_params=_params(("parallel", "arbitrary"), 48),
        name="sb_attn",
    )(q, kT, v)


def _out_proj_kernel(o_ref, h_ref, w_ref, g_ref, h1_ref, xn_ref):
    h1 = h_ref[...] + jnp.dot(o_ref[...], w_ref[...], preferred_element_type=F32)
    h1_ref[...] = h1
    xn_ref[...] = _rms_rows(h1, g_ref[...]).astype(BF16)


def _out_proj(o, h, layer, w, g, tm=256):
    S, D = h.shape
    tm = min(tm, S)
    return pl.pallas_call(
        _out_proj_kernel,
        grid=(S // tm,),
        in_specs=[
            pl.BlockSpec((tm, o.shape[1]), lambda i: (i, 0)),
            pl.BlockSpec((tm, D), lambda i: (i, 0)),
            _resident_layer(w, layer),
            _resident((1, D)),
        ],
        out_specs=[
            pl.BlockSpec((tm, D), lambda i: (i, 0)),
            pl.BlockSpec((tm, D), lambda i: (i, 0)),
        ],
        out_shape=[
            jax.ShapeDtypeStruct((S, D), F32),
            jax.ShapeDtypeStruct((S, D), BF16),
        ],
        compiler_params=_params(("parallel",), 48),
        name="out_proj",
    )(o, h, w, g)


def _mlp_up_kernel(x_ref, w_ref, u_ref):
    u = jnp.dot(x_ref[...], w_ref[...].astype(BF16), preferred_element_type=F32)
    u = jnp.maximum(u, 0.0)
    u_ref[...] = (u * u).astype(BF16)


def _mlp_up(xn, layer, w, tm=2048, tn=512):
    S, D = xn.shape
    F = w.shape[2]
    tm, tn = min(tm, S), min(tn, F)
    return pl.pallas_call(
        _mlp_up_kernel,
        grid=(S // tm, F // tn),
        in_specs=[
            pl.BlockSpec((tm, D), lambda i, j: (i, 0), pipeline_mode=pl.Buffered(1)),
            pl.BlockSpec((None, D, tn), lambda i, j: (layer, 0, j)),
        ],
        out_specs=pl.BlockSpec((tm, tn), lambda i, j: (i, j)),
        out_shape=jax.ShapeDtypeStruct((S, F), BF16),
        compiler_params=_params(("parallel", "arbitrary"), 52),
        name="mlp_up",
    )(xn, w)


def _mlp_down_kernel(u_ref, w_ref, r_ref, g_ref, o_ref, hg_ref, ss_ref):
    k = pl.program_id(2)

    @pl.when(k == 0)
    def _():
        o_ref[...] = r_ref[...]

    o_ref[...] += jnp.dot(u_ref[...], w_ref[...], preferred_element_type=F32)

    @pl.when(k == pl.num_programs(2) - 1)
    def _():
        o = o_ref[...]
        hg_ref[...] = (o * g_ref[...]).astype(BF16)
        ss_ref[...] = jnp.broadcast_to(jnp.sum(o * o, axis=-1, keepdims=True), ss_ref.shape)


def _mlp_down(u, layer, w, res, g, tm=1024, tn=1024, tk=2048):
    S, F = u.shape
    D = w.shape[2]
    tm, tn, tk = min(tm, S), min(tn, D), min(tk, F)
    return pl.pallas_call(
        _mlp_down_kernel,
        grid=(S // tm, D // tn, F // tk),
        in_specs=[
            pl.BlockSpec((tm, tk), lambda i, j, k: (i, k)),
            pl.BlockSpec((None, tk, tn), lambda i, j, k: (layer, k, j)),
            pl.BlockSpec((tm, tn), lambda i, j, k: (i, j)),
            pl.BlockSpec((1, tn), lambda i, j, k: (0, j)),
        ],
        out_specs=[
            pl.BlockSpec((tm, tn), lambda i, j, k: (i, j)),
            pl.BlockSpec((tm, tn), lambda i, j, k: (i, j)),
            pl.BlockSpec((tm, LANES), lambda i, j, k: (i, j)),
        ],
        out_shape=[
            jax.ShapeDtypeStruct((S, D), F32),
            jax.ShapeDtypeStruct((S, D), BF16),
            jax.ShapeDtypeStruct((S, (D // tn) * LANES), F32),
        ],
        compiler_params=_params(("parallel", "parallel", "arbitrary"), 48),
        name="mlp_down",
    )(u, w, res, g)


def _ple_kernel(hg_ref, ss_ref, h_ref, wg_ref, p_ref, wp_ref, o_ref, *, d_model):
    ss = ss_ref[...]
    total = ss[:, 0:1]
    for t in range(1, ss.shape[1] // LANES):
        total = total + ss[:, t * LANES:t * LANES + 1]
    inv = lax.rsqrt(total * (1.0 / d_model) + EPS)
    logits = jnp.dot(hg_ref[...], wg_ref[...], preferred_element_type=F32) * inv
    gate = 1.0 / (1.0 + jnp.exp(-logits))
    emb = jnp.dot(p_ref[...].astype(BF16), wp_ref[...], preferred_element_type=F32)
    o_ref[...] = h_ref[...] + emb * gate


def _ple(h, hg, ss, layer, wg, p, wp, tm=1024, tn=512):
    S, D = h.shape
    P = p.shape[-1]
    tm, tn = min(tm, S), min(tn, D)
    return pl.pallas_call(
        functools.partial(_ple_kernel, d_model=D),
        grid=(S // tm, D // tn),
        in_specs=[
            pl.BlockSpec((tm, D), lambda i, j: (i, 0)),
            pl.BlockSpec((tm, ss.shape[1]), lambda i, j: (i, 0)),
            pl.BlockSpec((tm, tn), lambda i, j: (i, j)),
            pl.BlockSpec((None, D, tn), lambda i, j: (layer, 0, j)),
            pl.BlockSpec((None, None, tm, P), lambda i, j: (layer, 0, i, 0)),
            pl.BlockSpec((None, P, tn), lambda i, j: (layer, 0, j)),
        ],
        out_specs=pl.BlockSpec((tm, tn), lambda i, j: (i, j)),
        out_shape=jax.ShapeDtypeStruct((S, D), F32),
        compiler_params=_params(("parallel", "arbitrary"), 48),
        name="ple_gate",
    )(hg, ss, h, wg, p, wp)


def _rope_tables(positions):
    inv_freq = ROPE_THETA ** (-jnp.arange(0, MLA_ROPE, 2, dtype=F32) / MLA_ROPE)
    ang = positions.astype(F32)[:, None] * inv_freq
    return jnp.cos(ang).T, jnp.sin(ang).T


def kernel(x, p, positions, norm_mix, norm_mlp, norm_ple, mla_w_in, mla_q_norm, mla_kv_norm, mla_w_uq, mla_w_ukv, mla_q_gain, mla_k_gain, mla_w_o, sb_w_qkv, sb_w_o, mlp_w_up, mlp_w_down, ple_w_proj, ple_w_gate):
    assert x.shape[0] == 1, "one sequence per call"
    depth = p.shape[0]
    cosT, sinT = _rope_tables(positions[0])
    col = lambda v: v.astype(F32)[:, None]
    row = lambda v: v.astype(F32)[None, :]
    transposed = lambda w: jnp.swapaxes(w, 1, 2).astype(BF16)
    mla_in_t, mla_uq_t, mla_ukv_t = transposed(mla_w_in), transposed(mla_w_uq), transposed(mla_w_ukv)
    mla_o, sb_qkv, sb_o = mla_w_o.astype(BF16), sb_w_qkv.astype(BF16), sb_w_o.astype(BF16)
    w_down, w_gate, w_proj = mlp_w_down.astype(BF16), ple_w_gate.astype(BF16), ple_w_proj.astype(BF16)
    h = x[0]
    for i in range(depth):
        j = i // N_MIXERS
        if i % N_MIXERS == 0:
            qT, k, vT = _mla_proj(
                h, row(norm_mix[i]), j, mla_in_t, col(mla_q_norm[j]), col(mla_kv_norm[j]),
                mla_uq_t, mla_ukv_t, col(mla_q_gain[j]), col(mla_k_gain[j]), cosT, sinT)
            o = _mla_attn(qT, k, vT)
            w_o = mla_o
        else:
            q, kT, v = _sb_proj(h, row(norm_mix[i]), j, sb_qkv)
            o = _sb_attn(q, kT, v)
            w_o = sb_o
        h1, xn = _out_proj(o, h, j, w_o, row(norm_mlp[i]))
        u = _mlp_up(xn, i, mlp_w_up)
        h2, hg, ss = _mlp_down(u, i, w_down, h1, row(norm_ple[i]))
        h = _ple(h2, hg, ss, i, w_gate, p, w_proj)
    return h[None]
```

```python
import functools

import jax
import jax.numpy as jnp
from jax import lax
from jax.experimental import pallas as pl
from jax.experimental.pallas import tpu as pltpu

F32 = jnp.float32
BF16 = jnp.bfloat16

CHUNK = 64
N_MIXERS = 2
MLA_HEADS = 8
MLA_Q_LORA = 1024
MLA_KV_LORA = 512
MLA_NOPE = 128
MLA_ROPE = 64
MLA_QK = MLA_NOPE + MLA_ROPE
MLA_QK_PAD = 256
MLA_V = 128
MLA_V_AUG = MLA_V + 16
ROPE_THETA = 10000.0
LOG2E = 1.4426950408889634
SB_HEADS = 4
SB_HEAD_DIM = 256
EPS = 1e-6

MIB = 1024 * 1024
V7X_VMEM_BYTES = 64 * MIB
LANES = 128


def _params(semantics, vmem_mib):
    assert vmem_mib * MIB < V7X_VMEM_BYTES
    return pltpu.CompilerParams(dimension_semantics=semantics, vmem_limit_bytes=vmem_mib * MIB)


def _resident(shape):
    return pl.BlockSpec(shape, lambda *_: (0,) * len(shape), pipeline_mode=pl.Buffered(1))


def _resident_layer(stack, layer):
    tail = stack.shape[1:]
    return pl.BlockSpec((None,) + tail, lambda *_: (layer,) + (0,) * len(tail),
                        pipeline_mode=pl.Buffered(1))


def _rms_rows(x, gain):
    return x * lax.rsqrt(jnp.mean(x * x, axis=-1, keepdims=True) + EPS) * gain


def _neg_abs(x):
    bits = lax.bitcast_convert_type(x, jnp.uint32) | jnp.uint32(0x80000000)
    return lax.bitcast_convert_type(bits, F32)


def _rms_cols(x, gain):
    return x * lax.rsqrt(jnp.mean(x * x, axis=0, keepdims=True) + EPS) * gain


def _mla_proj_kernel(h_ref, g_ref, w_in_ref, qn_ref, kvn_ref, w_uq_ref, w_ukv_ref,
                     qg_ref, kg_ref, cos_ref, sin_ref, qT_ref, k_ref, vT_ref,
                     cq_ref, ckv_ref, kr_ref):
    tm = h_ref.shape[0]
    tok = slice(0, tm)

    @pl.when(pl.program_id(0) == 0)
    def _():
        cq_ref[...] = jnp.zeros(cq_ref.shape, BF16)
        ckv_ref[...] = jnp.zeros(ckv_ref.shape, BF16)
        kr_ref[...] = jnp.zeros(kr_ref.shape, F32)

    _mla_heads(tok, cq_ref, ckv_ref, kr_ref, w_uq_ref, w_ukv_ref, qg_ref, kg_ref, cos_ref, sin_ref,
               qT_ref, k_ref, vT_ref)

    y = _rms_rows(h_ref[tok, :], g_ref[...]).astype(BF16)
    projT = lax.dot_general(w_in_ref[...], y, (((1,), (1,)), ((), ())),
                            preferred_element_type=F32)
    cq_ref[...] = _rms_cols(projT[:MLA_Q_LORA], qn_ref[...]).astype(BF16)
    ckv_ref[...] = _rms_cols(projT[MLA_Q_LORA:MLA_Q_LORA + MLA_KV_LORA], kvn_ref[...]).astype(BF16)
    kr_ref[...] = projT[MLA_Q_LORA + MLA_KV_LORA:]


def _mla_heads(tok, cq_ref, ckv_ref, kr_ref, w_uq_ref, w_ukv_ref, qg_ref, kg_ref, cos_ref, sin_ref,
               qT_ref, k_ref, vT_ref):
    tm = tok.stop - tok.start
    kr = kr_ref[...]
    qT = jnp.dot(w_uq_ref[...], cq_ref[...], preferred_element_type=F32)
    kvT = jnp.dot(w_ukv_ref[...], ckv_ref[...], preferred_element_type=F32)
    cos = cos_ref[:, tok]
    sin = sin_ref[:, tok]
    qg = qg_ref[...]
    kg = kg_ref[...]
    kr_ss = jnp.sum(kr * kr, axis=0, keepdims=True)
    scale = MLA_QK ** -0.5 * LOG2E
    half = MLA_ROPE // 2
    pad = MLA_QK_PAD - MLA_QK
    aug = MLA_V_AUG - MLA_V
    ones_tile = (lax.broadcasted_iota(jnp.int32, (aug, tm), 0) == 0).astype(F32).astype(BF16)
    for hd in range(MLA_HEADS):
        qh = _rms_cols(qT[hd * MLA_QK:(hd + 1) * MLA_QK], qg) * scale
        x1 = qh[MLA_NOPE:MLA_NOPE + half]
        x2 = qh[MLA_NOPE + half:]
        qT_ref[hd, 0:MLA_NOPE, tok] = qh[:MLA_NOPE].astype(BF16)
        qT_ref[hd, MLA_NOPE:MLA_NOPE + half, tok] = (x1 * cos - x2 * sin).astype(BF16)
        qT_ref[hd, MLA_NOPE + half:MLA_QK, tok] = (x2 * cos + x1 * sin).astype(BF16)
        qT_ref[hd, MLA_QK:, tok] = jnp.zeros((pad, tm), BF16)

        base = hd * (MLA_NOPE + MLA_V)
        kn = kvT[base:base + MLA_NOPE]
        inv = lax.rsqrt((jnp.sum(kn * kn, axis=0, keepdims=True) + kr_ss) * (1.0 / MLA_QK) + EPS)
        kn = kn * inv * kg[:MLA_NOPE]
        krh = kr * inv * kg[MLA_NOPE:]
        k1 = krh[:half]
        k2 = krh[half:]
        kT = jnp.concatenate(
            [kn, k1 * cos - k2 * sin, k2 * cos + k1 * sin, jnp.zeros((pad, tm), F32)], axis=0)
        k_ref[hd, tok, :] = kT.T.astype(BF16)
        vT_ref[hd, :MLA_V, tok] = kvT[base + MLA_NOPE:base + MLA_NOPE + MLA_V].astype(BF16)
        vT_ref[hd, MLA_V:, tok] = ones_tile


def _mla_proj(h, g, layer, w_inT, qn, kvn, w_uqT, w_ukvT, qg, kg, cosT, sinT, tm=256):
    S, D = h.shape
    tm = min(tm, S)
    H = MLA_HEADS
    n = S // tm
    latents = lambda s: jnp.minimum(s, n - 1)
    heads = lambda s: jnp.maximum(s - 1, 0)
    return pl.pallas_call(
        _mla_proj_kernel,
        grid=(n + 1,),
        in_specs=[
            pl.BlockSpec((tm, D), lambda s: (latents(s), 0)),
            _resident((1, D)),
            _resident_layer(w_inT, layer),
            _resident(qn.shape),
            _resident(kvn.shape),
            _resident_layer(w_uqT, layer),
            _resident_layer(w_ukvT, layer),
            _resident(qg.shape),
            _resident(kg.shape),
            pl.BlockSpec((MLA_ROPE // 2, tm), lambda s: (0, heads(s))),
            pl.BlockSpec((MLA_ROPE // 2, tm), lambda s: (0, heads(s))),
        ],
        out_specs=[
            pl.BlockSpec((H, MLA_QK_PAD, tm), lambda s: (0, 0, heads(s))),
            pl.BlockSpec((H, tm, MLA_QK_PAD), lambda s: (0, heads(s), 0)),
            pl.BlockSpec((H, MLA_V_AUG, tm), lambda s: (0, 0, heads(s))),
        ],
        out_shape=[
            jax.ShapeDtypeStruct((H, MLA_QK_PAD, S), BF16),
            jax.ShapeDtypeStruct((H, S, MLA_QK_PAD), BF16),
            jax.ShapeDtypeStruct((H, MLA_V_AUG, S), BF16),
        ],
        scratch_shapes=[
            pltpu.VMEM((MLA_Q_LORA, tm), BF16),
            pltpu.VMEM((MLA_KV_LORA, tm), BF16),
            pltpu.VMEM((MLA_ROPE, tm), F32),
        ],
        compiler_params=_params(("arbitrary",), 52),
        name="mla_proj",
    )(h, g, w_inT, qn, kvn, w_uqT, w_ukvT, qg, kg, cosT, sinT)


MLA_DEPTH = 4


def _mla_attn_kernel(qT_ref, k_ref, vT_ref, o_ref, s0_ref, s1_ref, s2_ref, s3_ref, p0_ref, p1_ref,
                     m_ref, alpha_ref, acc_ref, *, TK):
    i = pl.program_id(1)
    N = MLA_DEPTH
    TQ = N * TK
    s_refs = (s0_ref, s1_ref, s2_ref, s3_ref)
    p_refs = (p0_ref, p1_ref)

    def lo(u):
        return u * TK if 0 <= u < N else 0

    def scores(block, u):
        start = pl.multiple_of(block * TK, TK)
        s_refs[u % N][:, lo(u):] = jnp.dot(k_ref[0, pl.ds(start, TK), :], qT_ref[0, :, lo(u):],
                                           preferred_element_type=F32)

    def pv(block, u, trimmed=True):
        c = lo(u) if trimmed else 0
        start = pl.multiple_of(block * TK, TK)
        prod = jnp.dot(vT_ref[0, :, pl.ds(start, TK)], p_refs[u % 2][:, c:],
                       preferred_element_type=F32)
        acc_ref[:, c:] = alpha_ref[u % N, :, c:] * acc_ref[:, c:] + prod

    def running_max(u, diagonal):
        s_ref = s_refs[u % N]
        c = lo(u)
        if diagonal:
            kc = lax.broadcasted_iota(jnp.int32, (TK, TK), 0) // CHUNK
            qc = lax.broadcasted_iota(jnp.int32, (TK, TK), 1) // CHUNK
            s_ref[:, c:c + TK] = jnp.where(kc <= qc, s_ref[:, c:c + TK], -jnp.inf)
        m_prev = m_ref[(u - 1) % N, :, c:]
        m_new = jnp.maximum(m_prev, jnp.max(s_ref[:, c:], axis=0, keepdims=True))
        m_ref[u % N, :, c:] = m_new
        alpha_ref[u % N, :, c:] = jnp.exp2(m_prev - m_new)
        if c:
            m_ref[u % N, :, :c] = m_ref[(u - 1) % N, :, :c]
            alpha_ref[u % N, :, :c] = jnp.ones((1, c), F32)

    def probabilities(u, zero_hidden=False):
        c = lo(u)
        p_refs[u % 2][:, c:] = jnp.exp2(s_refs[u % N][:, c:] - m_ref[u % N, :, c:]).astype(BF16)
        if zero_hidden and c:
            p_refs[u % 2][:, :c] = jnp.zeros((TK, c), BF16)

    def step(u, block_ahead, block_behind, diagonal_next=False, last_diagonal=False):
        scores(block_ahead, u + 2)
        pv(block_behind, u - 1, trimmed=not (u - 1 == N - 1))
        probabilities(u, zero_hidden=last_diagonal)
        running_max(u + 1, diagonal_next)

    first = N * i
    acc_ref[...] = jnp.zeros(acc_ref.shape, F32)
    p_refs[1][...] = jnp.zeros(p_refs[1].shape, BF16)
    alpha_ref[N - 1] = jnp.ones(alpha_ref.shape[1:], F32)
    m_ref[N - 1] = jnp.full(m_ref.shape[1:], -jnp.inf, F32)
    scores(first, 0)
    scores(first + 1, 1)
    running_max(0, True)

    step(0, first + 2, first, diagonal_next=True)
    step(1, first + 3, first, diagonal_next=True)
    step(2, 0, first + 1, diagonal_next=True)
    step(3, 1, first + 2, last_diagonal=True)

    def body(j, carry):
        base = N * j
        step(N, base + 2, jnp.where(j == 0, first + 3, base - 1))
        step(N + 1, base + 3, base)
        step(N + 2, base + 4, base + 1)
        step(N + 3, base + 5, base + 2)
        return carry

    lax.fori_loop(0, i, body, 0)
    pv(jnp.where(i == 0, first + 3, first - 1), N - 1, trimmed=False)
    o = acc_ref[:MLA_V, :] / acc_ref[MLA_V:MLA_V + 1, :]
    o_ref[...] = o.T.astype(BF16)


def _mla_attn(qT, k, vT, tk=512):
    H, _, S = qT.shape
    tk = min(tk, S // MLA_DEPTH)
    tq = MLA_DEPTH * tk
    return pl.pallas_call(
        functools.partial(_mla_attn_kernel, TK=tk),
        grid=(H, S // tq),
        in_specs=[
            pl.BlockSpec((1, MLA_QK_PAD, tq), lambda h, i: (h, 0, i)),
            pl.BlockSpec((1, S, MLA_QK_PAD), lambda h, i: (h, 0, 0)),
            pl.BlockSpec((1, MLA_V_AUG, S), lambda h, i: (h, 0, 0)),
        ],
        out_specs=pl.BlockSpec((tq, MLA_V), lambda h, i: (i, h)),
        out_shape=jax.ShapeDtypeStruct((S, H * MLA_V), BF16),
        scratch_shapes=(
            [pltpu.VMEM((tk, tq), F32)] * MLA_DEPTH
            + [pltpu.VMEM((tk, tq), BF16)] * 2
            + [pltpu.VMEM((MLA_DEPTH, 1, tq), F32),
               pltpu.VMEM((MLA_DEPTH, 1, tq), F32),
               pltpu.VMEM((MLA_V_AUG, tq), F32)]
        ),
        compiler_params=_params(("parallel", "arbitrary"), 54),
        name="mla_attn",
    )(qT, k, vT)


def _sb_proj_kernel(h_ref, g_ref, w_ref, q_ref, kT_ref, v_ref):
    HD = SB_HEADS * SB_HEAD_DIM
    y = _rms_rows(h_ref[...], g_ref[...]).astype(BF16)
    qkv = jnp.dot(y, w_ref[...], preferred_element_type=F32)
    q_ref[...] = (qkv[:, :HD] * (SB_HEAD_DIM ** -0.5 * LOG2E)).astype(BF16)
    v_ref[...] = qkv[:, 2 * HD:].astype(BF16)
    for hd in range(SB_HEADS):
        k = qkv[:, HD + hd * SB_HEAD_DIM:HD + (hd + 1) * SB_HEAD_DIM]
        kT_ref[hd] = k.T.astype(BF16)


def _sb_proj(h, g, layer, w, tm=256):
    S, D = h.shape
    tm = min(tm, S)
    HD = SB_HEADS * SB_HEAD_DIM
    return pl.pallas_call(
        _sb_proj_kernel,
        grid=(S // tm,),
        in_specs=[
            pl.BlockSpec((tm, D), lambda i: (i, 0)),
            _resident((1, D)),
            _resident_layer(w, layer),
        ],
        out_specs=[
            pl.BlockSpec((tm, HD), lambda i: (i, 0)),
            pl.BlockSpec((SB_HEADS, SB_HEAD_DIM, tm), lambda i: (0, 0, i)),
            pl.BlockSpec((tm, HD), lambda i: (i, 0)),
        ],
        out_shape=[
            jax.ShapeDtypeStruct((S, HD), BF16),
            jax.ShapeDtypeStruct((SB_HEADS, SB_HEAD_DIM, S), BF16),
            jax.ShapeDtypeStruct((S, HD), BF16),
        ],
        compiler_params=_params(("parallel",), 52),
        name="sb_proj",
    )(h, g, w)


def _sb_attn_kernel(q_ref, kT_ref, v_ref, o_ref, za_ref, zb_ref, aa_ref, ab_ref,
                    acc_ref, after_ref, later_ref, *, TK, R):
    i = pl.program_id(1)
    row = lax.broadcasted_iota(jnp.int32, (TK, TK), 0)
    col = lax.broadcasted_iota(jnp.int32, (TK, TK), 1)
    later_ref[...] = (row > col).astype(F32).astype(BF16)
    last = R * i + R - 1

    def key_start(t):
        return pl.multiple_of(jnp.maximum(last - t, 0) * TK, TK)

    def logits(t, z_ref):
        z_ref[...] = jnp.dot(q_ref[...], kT_ref[0, :, pl.ds(key_start(t), TK)],
                             preferred_element_type=F32)

    def pending_av(t, a_ref):
        acc_ref[...] += jnp.dot(a_ref[...], v_ref[pl.ds(key_start(jnp.maximum(t, 0)), TK), :],
                                preferred_element_type=F32)

    def weights(z_ref, a_ref, mask):
        for r in range(R):
            rows = slice(r * TK, (r + 1) * TK)
            kind = None if mask is None else mask[r]
            if kind == "none":
                a_ref[rows, :] = jnp.zeros((TK, TK), BF16)
                continue
            z = z_ref[rows, :]
            sp = jnp.maximum(z, 0.0) + jnp.log2(1.0 + jnp.exp2(_neg_abs(z)))
            if kind == "diag":
                valid = col < row
                sp = jnp.where(valid, sp, 0.0)
            within = jnp.dot(sp.astype(BF16), later_ref[...], preferred_element_type=F32)
            after = after_ref[rows, :]
            a = jnp.exp2(z - sp - within - after)
            if kind == "diag":
                a = jnp.where(valid, a, 0.0)
            a_ref[rows, :] = a.astype(BF16)
            after_ref[rows, :] = after + jnp.sum(sp, axis=1, keepdims=True)

    def pair(t, masks):
        logits(t + 1, zb_ref)
        pending_av(t - 1, ab_ref)
        weights(za_ref, aa_ref, masks[0])
        logits(t + 2, za_ref)
        pending_av(t, aa_ref)
        weights(zb_ref, ab_ref, masks[1])

    acc_ref[...] = jnp.zeros(acc_ref.shape, F32)
    after_ref[...] = jnp.zeros(after_ref.shape, F32)
    ab_ref[...] = jnp.zeros(ab_ref.shape, BF16)
    logits(0, za_ref)

    def diagonal_mask(t):
        beside = R - 1 - t
        return tuple("diag" if r == beside else ("none" if r < beside else None) for r in range(R))

    for t in range(0, R, 2):
        pair(t, (diagonal_mask(t), diagonal_mask(t + 1)))

    def body(j, carry):
        pair(2 * j + R, (None, None))
        return carry

    lax.fori_loop(0, (R // 2) * i, body, 0)
    pending_av(last, ab_ref)
    o_ref[...] = acc_ref[...].astype(BF16)


def _sb_attn(q, kT, v, tk=256, r=4):
    S = q.shape[0]
    tk = min(tk, S // r)
    tq = r * tk
    Dh = SB_HEAD_DIM
    return pl.pallas_call(
        functools.partial(_sb_attn_kernel, TK=tk, R=r),
        grid=(SB_HEADS, S // tq),
        in_specs=[
            pl.BlockSpec((tq, Dh), lambda h, i: (i, h)),
            pl.BlockSpec((1, Dh, S), lambda h, i: (h, 0, 0)),
            pl.BlockSpec((S, Dh), lambda h, i: (0, h)),
        ],
        out_specs=pl.BlockSpec((tq, Dh), lambda h, i: (i, h)),
        out_shape=jax.ShapeDtypeStruct((S, SB_HEADS * Dh), BF16),
        scratch_shapes=[
            pltpu.VMEM((tq, tk), F32),
            pltpu.VMEM((tq, tk), F32),
            pltpu.VMEM((tq, tk), BF16),
            pltpu.VMEM((tq, tk), BF16),
            pltpu.VMEM((tq, Dh), F32),
            pltpu.VMEM((tq, 1), F32),
            pltpu.VMEM((tk, tk), BF16),
        ],
        compiler_params=_params(("parallel", "arbitrary"), 48),
        name="sb_attn",
    )(q, kT, v)


def _out_proj_kernel(o_ref, h_ref, w_ref, g_ref, h1_ref, xn_ref):
    h1 = h_ref[...] + jnp.dot(o_ref[...], w_ref[...], preferred_element_type=F32)
    h1_ref[...] = h1
    xn_ref[...] = _rms_rows(h1, g_ref[...]).astype(BF16)


def _out_proj(o, h, layer, w, g, tm=256):
    S, D = h.shape
    tm = min(tm, S)
    return pl.pallas_call(
        _out_proj_kernel,
        grid=(S // tm,),
        in_specs=[
            pl.BlockSpec((tm, o.shape[1]), lambda i: (i, 0)),
            pl.BlockSpec((tm, D), lambda i: (i, 0)),
            _resident_layer(w, layer),
            _resident((1, D)),
        ],
        out_specs=[
            pl.BlockSpec((tm, D), lambda i: (i, 0)),
            pl.BlockSpec((tm, D), lambda i: (i, 0)),
        ],
        out_shape=[
            jax.ShapeDtypeStruct((S, D), F32),
            jax.ShapeDtypeStruct((S, D), BF16),
        ],
        compiler_params=_params(("parallel",), 48),
        name="out_proj",
    )(o, h, w, g)


def _mlp_up_kernel(x_ref, w_ref, wd_ref, wg_ref, u_ref, wd_bf16_ref, wg_bf16_ref):
    u = jnp.dot(x_ref[...], w_ref[...].astype(BF16), preferred_element_type=F32)
    u = jnp.maximum(u, 0.0)
    u_ref[...] = (u * u).astype(BF16)
    wd_bf16_ref[...] = wd_ref[...].astype(BF16)
    wg_bf16_ref[...] = wg_ref[...].astype(BF16)


def _mlp_up(xn, layer, w, w_down, w_gate, tm=2048, tn=512):
    S, D = xn.shape
    F = w.shape[2]
    tm, tn = min(tm, S), min(tn, F)
    ni, nj = S // tm, F // tn
    steps = ni * nj
    rows_d, rows_g = F // steps, D // steps
    assert rows_d * steps == F and rows_g * steps == D and rows_g % 16 == 0
    slab = lambda i, j: i * nj + j
    return pl.pallas_call(
        _mlp_up_kernel,
        grid=(ni, nj),
        in_specs=[
            pl.BlockSpec((tm, D), lambda i, j: (i, 0), pipeline_mode=pl.Buffered(1)),
            pl.BlockSpec((None, D, tn), lambda i, j: (layer, 0, j)),
            pl.BlockSpec((None, rows_d, D), lambda i, j: (layer, slab(i, j), 0)),
            pl.BlockSpec((None, rows_g, D), lambda i, j: (layer, slab(i, j), 0)),
        ],
        out_specs=[
            pl.BlockSpec((tm, tn), lambda i, j: (i, j)),
            pl.BlockSpec((rows_d, D), lambda i, j: (slab(i, j), 0)),
            pl.BlockSpec((rows_g, D), lambda i, j: (slab(i, j), 0)),
        ],
        out_shape=[
            jax.ShapeDtypeStruct((S, F), BF16),
            jax.ShapeDtypeStruct((F, D), BF16),
            jax.ShapeDtypeStruct((D, D), BF16),
        ],
        compiler_params=_params(("parallel", "arbitrary"), 52),
        name="mlp_up",
    )(xn, w, w_down, w_gate)


def _mlp_down_kernel(u_ref, w_ref, r_ref, g_ref, o_ref, hg_ref, ss_ref):
    k = pl.program_id(2)

    @pl.when(k == 0)
    def _():
        o_ref[...] = r_ref[...]

    o_ref[...] += jnp.dot(u_ref[...], w_ref[...], preferred_element_type=F32)

    @pl.when(k == pl.num_programs(2) - 1)
    def _():
        o = o_ref[...]
        hg_ref[...] = (o * g_ref[...]).astype(BF16)
        ss_ref[...] = jnp.broadcast_to(jnp.sum(o * o, axis=-1, keepdims=True), ss_ref.shape)


def _mlp_down(u, w, res, g, tm=1024, tn=1024, tk=2048):
    S, F = u.shape
    D = w.shape[1]
    tm, tn, tk = min(tm, S), min(tn, D), min(tk, F)
    return pl.pallas_call(
        _mlp_down_kernel,
        grid=(S // tm, D // tn, F // tk),
        in_specs=[
            pl.BlockSpec((tm, tk), lambda i, j, k: (i, k)),
            pl.BlockSpec((tk, tn), lambda i, j, k: (k, j)),
            pl.BlockSpec((tm, tn), lambda i, j, k: (i, j)),
            pl.BlockSpec((1, tn), lambda i, j, k: (0, j)),
        ],
        out_specs=[
            pl.BlockSpec((tm, tn), lambda i, j, k: (i, j)),
            pl.BlockSpec((tm, tn), lambda i, j, k: (i, j)),
            pl.BlockSpec((tm, LANES), lambda i, j, k: (i, j)),
        ],
        out_shape=[
            jax.ShapeDtypeStruct((S, D), F32),
            jax.ShapeDtypeStruct((S, D), BF16),
            jax.ShapeDtypeStruct((S, (D // tn) * LANES), F32),
        ],
        compiler_params=_params(("parallel", "parallel", "arbitrary"), 48),
        name="mlp_down",
    )(u, w, res, g)


def _ple_kernel(hg_ref, ss_ref, h_ref, wg_ref, p_ref, wp_ref, o_ref, *, d_model):
    ss = ss_ref[...]
    total = ss[:, 0:1]
    for t in range(1, ss.shape[1] // LANES):
        total = total + ss[:, t * LANES:t * LANES + 1]
    inv = lax.rsqrt(total * (1.0 / d_model) + EPS)
    logits = jnp.dot(hg_ref[...], wg_ref[...], preferred_element_type=F32) * inv
    gate = 1.0 / (1.0 + jnp.exp(-logits))
    emb = jnp.dot(p_ref[...].astype(BF16), wp_ref[...], preferred_element_type=F32)
    o_ref[...] = h_ref[...] + emb * gate


def _ple(h, hg, ss, layer, wg, p, wp, tm=1024, tn=512):
    S, D = h.shape
    P = p.shape[-1]
    tm, tn = min(tm, S), min(tn, D)
    return pl.pallas_call(
        functools.partial(_ple_kernel, d_model=D),
        grid=(S // tm, D // tn),
        in_specs=[
            pl.BlockSpec((tm, D), lambda i, j: (i, 0)),
            pl.BlockSpec((tm, ss.shape[1]), lambda i, j: (i, 0)),
            pl.BlockSpec((tm, tn), lambda i, j: (i, j)),
            pl.BlockSpec((D, tn), lambda i, j: (0, j)),
            pl.BlockSpec((None, None, tm, P), lambda i, j: (layer, 0, i, 0)),
            pl.BlockSpec((None, P, tn), lambda i, j: (layer, 0, j)),
        ],
        out_specs=pl.BlockSpec((tm, tn), lambda i, j: (i, j)),
        out_shape=jax.ShapeDtypeStruct((S, D), F32),
        compiler_params=_params(("parallel", "arbitrary"), 48),
        name="ple_gate",
    )(hg, ss, h, wg, p, wp)


def _rope_tables(positions):
    inv_freq = ROPE_THETA ** (-jnp.arange(0, MLA_ROPE, 2, dtype=F32) / MLA_ROPE)
    ang = positions.astype(F32)[:, None] * inv_freq
    return jnp.cos(ang).T, jnp.sin(ang).T


def kernel(x, p, positions, norm_mix, norm_mlp, norm_ple, mla_w_in, mla_q_norm, mla_kv_norm, mla_w_uq, mla_w_ukv, mla_q_gain, mla_k_gain, mla_w_o, sb_w_qkv, sb_w_o, mlp_w_up, mlp_w_down, ple_w_proj, ple_w_gate):
    assert x.shape[0] == 1, "one sequence per call"
    depth = p.shape[0]
    cosT, sinT = _rope_tables(positions[0])
    col = lambda v: v.astype(F32)[:, None]
    row = lambda v: v.astype(F32)[None, :]
    transposed = lambda w: jnp.swapaxes(w, 1, 2).astype(BF16)
    mla_in_t, mla_uq_t, mla_ukv_t = transposed(mla_w_in), transposed(mla_w_uq), transposed(mla_w_ukv)
    mla_o, sb_qkv, sb_o = mla_w_o.astype(BF16), sb_w_qkv.astype(BF16), sb_w_o.astype(BF16)
    w_proj = ple_w_proj.astype(BF16)
    h = x[0]
    for i in range(depth):
        j = i // N_MIXERS
        if i % N_MIXERS == 0:
            qT, k, vT = _mla_proj(
                h, row(norm_mix[i]), j, mla_in_t, col(mla_q_norm[j]), col(mla_kv_norm[j]),
                mla_uq_t, mla_ukv_t, col(mla_q_gain[j]), col(mla_k_gain[j]), cosT, sinT)
            o = _mla_attn(qT, k, vT)
            w_o = mla_o
        else:
            q, kT, v = _sb_proj(h, row(norm_mix[i]), j, sb_qkv)
            o = _sb_attn(q, kT, v)
            w_o = sb_o
        h1, xn = _out_proj(o, h, j, w_o, row(norm_mlp[i]))
        u, w_down, w_gate = _mlp_up(xn, i, mlp_w_up, mlp_w_down, ple_w_gate)
        h2, hg, ss = _mlp_down(u, w_down, h1, row(norm_ple[i]))
        h = _ple(h2, hg, ss, i, w_gate, p, w_proj)
    return h[None]
```

```python
import functools

import jax
import jax.numpy as jnp
from jax import lax
from jax.experimental import pallas as pl
from jax.experimental.pallas import tpu as pltpu

F32 = jnp.float32
BF16 = jnp.bfloat16

CHUNK = 64
N_MIXERS = 2
MLA_HEADS = 8
MLA_Q_LORA = 1024
MLA_KV_LORA = 512
MLA_NOPE = 128
MLA_ROPE = 64
MLA_QK = MLA_NOPE + MLA_ROPE
MLA_QK_PAD = 256
MLA_V = 128
MLA_V_AUG = MLA_V + 16
ROPE_THETA = 10000.0
LOG2E = 1.4426950408889634
SB_HEADS = 4
SB_HEAD_DIM = 256
EPS = 1e-6

MIB = 1024 * 1024
V7X_VMEM_BYTES = 64 * MIB
LANES = 128


def _params(semantics, vmem_mib):
    assert vmem_mib * MIB < V7X_VMEM_BYTES
    return pltpu.CompilerParams(dimension_semantics=semantics, vmem_limit_bytes=vmem_mib * MIB)


def _resident(shape):
    return pl.BlockSpec(shape, lambda *_: (0,) * len(shape), pipeline_mode=pl.Buffered(1))


def _resident_layer(stack, layer):
    tail = stack.shape[1:]
    return pl.BlockSpec((None,) + tail, lambda *_: (layer,) + (0,) * len(tail),
                        pipeline_mode=pl.Buffered(1))


def _rms_rows(x, gain):
    return x * lax.rsqrt(jnp.mean(x * x, axis=-1, keepdims=True) + EPS) * gain


def _neg_abs(x):
    bits = lax.bitcast_convert_type(x, jnp.uint32) | jnp.uint32(0x80000000)
    return lax.bitcast_convert_type(bits, F32)


def _rms_cols(x, gain):
    return x * lax.rsqrt(jnp.mean(x * x, axis=0, keepdims=True) + EPS) * gain


def _mla_proj_kernel(h_ref, g_ref, w_in_ref, qn_ref, kvn_ref, w_uq_ref, w_ukv_ref,
                     qg_ref, kg_ref, cos_ref, sin_ref, qT_ref, k_ref, vT_ref,
                     cq_ref, ckv_ref, kr_ref):
    tm = h_ref.shape[0]
    tok = slice(0, tm)

    @pl.when(pl.program_id(0) == 0)
    def _():
        cq_ref[...] = jnp.zeros(cq_ref.shape, BF16)
        ckv_ref[...] = jnp.zeros(ckv_ref.shape, BF16)
        kr_ref[...] = jnp.zeros(kr_ref.shape, F32)

    _mla_heads(tok, cq_ref, ckv_ref, kr_ref, w_uq_ref, w_ukv_ref, qg_ref, kg_ref, cos_ref, sin_ref,
               qT_ref, k_ref, vT_ref)

    y = _rms_rows(h_ref[tok, :], g_ref[...]).astype(BF16)
    projT = lax.dot_general(w_in_ref[...], y, (((1,), (1,)), ((), ())),
                            preferred_element_type=F32)
    cq_ref[...] = _rms_cols(projT[:MLA_Q_LORA], qn_ref[...]).astype(BF16)
    ckv_ref[...] = _rms_cols(projT[MLA_Q_LORA:MLA_Q_LORA + MLA_KV_LORA], kvn_ref[...]).astype(BF16)
    kr_ref[...] = projT[MLA_Q_LORA + MLA_KV_LORA:]


def _mla_heads(tok, cq_ref, ckv_ref, kr_ref, w_uq_ref, w_ukv_ref, qg_ref, kg_ref, cos_ref, sin_ref,
               qT_ref, k_ref, vT_ref):
    tm = tok.stop - tok.start
    kr = kr_ref[...]
    qT = jnp.dot(w_uq_ref[...], cq_ref[...], preferred_element_type=F32)
    kvT = jnp.dot(w_ukv_ref[...], ckv_ref[...], preferred_element_type=F32)
    cos = cos_ref[:, tok]
    sin = sin_ref[:, tok]
    qg = qg_ref[...]
    kg = kg_ref[...]
    kr_ss = jnp.sum(kr * kr, axis=0, keepdims=True)
    scale = MLA_QK ** -0.5 * LOG2E
    half = MLA_ROPE // 2
    pad = MLA_QK_PAD - MLA_QK
    aug = MLA_V_AUG - MLA_V
    ones_tile = (lax.broadcasted_iota(jnp.int32, (aug, tm), 0) == 0).astype(F32).astype(BF16)
    for hd in range(MLA_HEADS):
        qh = _rms_cols(qT[hd * MLA_QK:(hd + 1) * MLA_QK], qg) * scale
        x1 = qh[MLA_NOPE:MLA_NOPE + half]
        x2 = qh[MLA_NOPE + half:]
        qT_ref[hd, 0:MLA_NOPE, tok] = qh[:MLA_NOPE].astype(BF16)
        qT_ref[hd, MLA_NOPE:MLA_NOPE + half, tok] = (x1 * cos - x2 * sin).astype(BF16)
        qT_ref[hd, MLA_NOPE + half:MLA_QK, tok] = (x2 * cos + x1 * sin).astype(BF16)
        qT_ref[hd, MLA_QK:, tok] = jnp.zeros((pad, tm), BF16)

        base = hd * (MLA_NOPE + MLA_V)
        kn = kvT[base:base + MLA_NOPE]
        inv = lax.rsqrt((jnp.sum(kn * kn, axis=0, keepdims=True) + kr_ss) * (1.0 / MLA_QK) + EPS)
        kn = kn * inv * kg[:MLA_NOPE]
        krh = kr * inv * kg[MLA_NOPE:]
        k1 = krh[:half]
        k2 = krh[half:]
        kT = jnp.concatenate(
            [kn, k1 * cos - k2 * sin, k2 * cos + k1 * sin, jnp.zeros((pad, tm), F32)], axis=0)
        k_ref[hd, tok, :] = kT.T.astype(BF16)
        vT_ref[hd, :MLA_V, tok] = kvT[base + MLA_NOPE:base + MLA_NOPE + MLA_V].astype(BF16)
        vT_ref[hd, MLA_V:, tok] = ones_tile


def _mla_proj(h, g, layer, w_inT, qn, kvn, w_uqT, w_ukvT, qg, kg, cosT, sinT, tm=256):
    S, D = h.shape
    tm = min(tm, S)
    H = MLA_HEADS
    n = S // tm
    latents = lambda s: jnp.minimum(s, n - 1)
    heads = lambda s: jnp.maximum(s - 1, 0)
    return pl.pallas_call(
        _mla_proj_kernel,
        grid=(n + 1,),
        in_specs=[
            pl.BlockSpec((tm, D), lambda s: (latents(s), 0)),
            _resident((1, D)),
            _resident_layer(w_inT, layer),
            _resident(qn.shape),
            _resident(kvn.shape),
            _resident_layer(w_uqT, layer),
            _resident_layer(w_ukvT, layer),
            _resident(qg.shape),
            _resident(kg.shape),
            pl.BlockSpec((MLA_ROPE // 2, tm), lambda s: (0, heads(s))),
            pl.BlockSpec((MLA_ROPE // 2, tm), lambda s: (0, heads(s))),
        ],
        out_specs=[
            pl.BlockSpec((H, MLA_QK_PAD, tm), lambda s: (0, 0, heads(s))),
            pl.BlockSpec((H, tm, MLA_QK_PAD), lambda s: (0, heads(s), 0)),
            pl.BlockSpec((H, MLA_V_AUG, tm), lambda s: (0, 0, heads(s))),
        ],
        out_shape=[
            jax.ShapeDtypeStruct((H, MLA_QK_PAD, S), BF16),
            jax.ShapeDtypeStruct((H, S, MLA_QK_PAD), BF16),
            jax.ShapeDtypeStruct((H, MLA_V_AUG, S), BF16),
        ],
        scratch_shapes=[
            pltpu.VMEM((MLA_Q_LORA, tm), BF16),
            pltpu.VMEM((MLA_KV_LORA, tm), BF16),
            pltpu.VMEM((MLA_ROPE, tm), F32),
        ],
        compiler_params=_params(("arbitrary",), 52),
        name="mla_proj",
    )(h, g, w_inT, qn, kvn, w_uqT, w_ukvT, qg, kg, cosT, sinT)


MLA_DEPTH = 4


def _mla_attn_kernel(qT_ref, k_ref, vT_ref, o_ref, s0_ref, s1_ref, s2_ref, s3_ref, p0_ref, p1_ref,
                     m_ref, alpha_ref, acc_ref, *, TK):
    i = pl.program_id(1)
    N = MLA_DEPTH
    TQ = N * TK
    s_refs = (s0_ref, s1_ref, s2_ref, s3_ref)
    p_refs = (p0_ref, p1_ref)

    def lo(u):
        return u * TK if 0 <= u < N else 0

    def scores(block, u):
        start = pl.multiple_of(block * TK, TK)
        s_refs[u % N][:, lo(u):] = jnp.dot(k_ref[0, pl.ds(start, TK), :], qT_ref[0, :, lo(u):],
                                           preferred_element_type=F32)

    def pv(block, u, trimmed=True):
        c = lo(u) if trimmed else 0
        start = pl.multiple_of(block * TK, TK)
        prod = jnp.dot(vT_ref[0, :, pl.ds(start, TK)], p_refs[u % 2][:, c:],
                       preferred_element_type=F32)
        acc_ref[:, c:] = alpha_ref[u % N, :, c:] * acc_ref[:, c:] + prod

    def running_max(u, diagonal):
        s_ref = s_refs[u % N]
        c = lo(u)
        if diagonal:
            kc = lax.broadcasted_iota(jnp.int32, (TK, TK), 0) // CHUNK
            qc = lax.broadcasted_iota(jnp.int32, (TK, TK), 1) // CHUNK
            s_ref[:, c:c + TK] = jnp.where(kc <= qc, s_ref[:, c:c + TK], -jnp.inf)
        m_prev = m_ref[(u - 1) % N, :, c:]
        m_new = jnp.maximum(m_prev, jnp.max(s_ref[:, c:], axis=0, keepdims=True))
        m_ref[u % N, :, c:] = m_new
        alpha_ref[u % N, :, c:] = jnp.exp2(m_prev - m_new)
        if c:
            m_ref[u % N, :, :c] = m_ref[(u - 1) % N, :, :c]
            alpha_ref[u % N, :, :c] = jnp.ones((1, c), F32)

    def probabilities(u, zero_hidden=False):
        c = lo(u)
        p_refs[u % 2][:, c:] = jnp.exp2(s_refs[u % N][:, c:] - m_ref[u % N, :, c:]).astype(BF16)
        if zero_hidden and c:
            p_refs[u % 2][:, :c] = jnp.zeros((TK, c), BF16)

    def step(u, block_ahead, block_behind, diagonal_next=False, last_diagonal=False):
        scores(block_ahead, u + 2)
        pv(block_behind, u - 1, trimmed=not (u - 1 == N - 1))
        probabilities(u, zero_hidden=last_diagonal)
        running_max(u + 1, diagonal_next)

    first = N * i
    acc_ref[...] = jnp.zeros(acc_ref.shape, F32)
    p_refs[1][...] = jnp.zeros(p_refs[1].shape, BF16)
    alpha_ref[N - 1] = jnp.ones(alpha_ref.shape[1:], F32)
    m_ref[N - 1] = jnp.full(m_ref.shape[1:], -jnp.inf, F32)
    scores(first, 0)
    scores(first + 1, 1)
    running_max(0, True)

    step(0, first + 2, first, diagonal_next=True)
    step(1, first + 3, first, diagonal_next=True)
    step(2, 0, first + 1, diagonal_next=True)
    step(3, 1, first + 2, last_diagonal=True)

    def four_steps(j):
        base = N * j
        step(N, base + 2, jnp.where(j == 0, first + 3, base - 1))
        step(N + 1, base + 3, base)
        step(N + 2, base + 4, base + 1)
        step(N + 3, base + 5, base + 2)

    def body(jj, carry):
        four_steps(2 * jj)
        four_steps(2 * jj + 1)
        return carry

    lax.fori_loop(0, i // 2, body, 0)

    @pl.when(i % 2 == 1)
    def _():
        four_steps(i - 1)
    pv(jnp.where(i == 0, first + 3, first - 1), N - 1, trimmed=False)
    o = acc_ref[:MLA_V, :] / acc_ref[MLA_V:MLA_V + 1, :]
    o_ref[...] = o.T.astype(BF16)


def _mla_attn(qT, k, vT, tk=512):
    H, _, S = qT.shape
    tk = min(tk, S // MLA_DEPTH)
    tq = MLA_DEPTH * tk
    return pl.pallas_call(
        functools.partial(_mla_attn_kernel, TK=tk),
        grid=(H, S // tq),
        in_specs=[
            pl.BlockSpec((1, MLA_QK_PAD, tq), lambda h, i: (h, 0, i)),
            pl.BlockSpec((1, S, MLA_QK_PAD), lambda h, i: (h, 0, 0)),
            pl.BlockSpec((1, MLA_V_AUG, S), lambda h, i: (h, 0, 0)),
        ],
        out_specs=pl.BlockSpec((tq, MLA_V), lambda h, i: (i, h)),
        out_shape=jax.ShapeDtypeStruct((S, H * MLA_V), BF16),
        scratch_shapes=(
            [pltpu.VMEM((tk, tq), F32)] * MLA_DEPTH
            + [pltpu.VMEM((tk, tq), BF16)] * 2
            + [pltpu.VMEM((MLA_DEPTH, 1, tq), F32),
               pltpu.VMEM((MLA_DEPTH, 1, tq), F32),
               pltpu.VMEM((MLA_V_AUG, tq), F32)]
        ),
        compiler_params=_params(("parallel", "arbitrary"), 54),
        name="mla_attn",
    )(qT, k, vT)


def _sb_proj_kernel(h_ref, g_ref, w_ref, q_ref, kT_ref, v_ref):
    HD = SB_HEADS * SB_HEAD_DIM
    y = _rms_rows(h_ref[...], g_ref[...]).astype(BF16)
    qkv = jnp.dot(y, w_ref[...], preferred_element_type=F32)
    q_ref[...] = (qkv[:, :HD] * (SB_HEAD_DIM ** -0.5 * LOG2E)).astype(BF16)
    v_ref[...] = qkv[:, 2 * HD:].astype(BF16)
    for hd in range(SB_HEADS):
        k = qkv[:, HD + hd * SB_HEAD_DIM:HD + (hd + 1) * SB_HEAD_DIM]
        kT_ref[hd] = k.T.astype(BF16)


def _sb_proj(h, g, layer, w, tm=256):
    S, D = h.shape
    tm = min(tm, S)
    HD = SB_HEADS * SB_HEAD_DIM
    return pl.pallas_call(
        _sb_proj_kernel,
        grid=(S // tm,),
        in_specs=[
            pl.BlockSpec((tm, D), lambda i: (i, 0)),
            _resident((1, D)),
            _resident_layer(w, layer),
        ],
        out_specs=[
            pl.BlockSpec((tm, HD), lambda i: (i, 0)),
            pl.BlockSpec((SB_HEADS, SB_HEAD_DIM, tm), lambda i: (0, 0, i)),
            pl.BlockSpec((tm, HD), lambda i: (i, 0)),
        ],
        out_shape=[
            jax.ShapeDtypeStruct((S, HD), BF16),
            jax.ShapeDtypeStruct((SB_HEADS, SB_HEAD_DIM, S), BF16),
            jax.ShapeDtypeStruct((S, HD), BF16),
        ],
        compiler_params=_params(("parallel",), 52),
        name="sb_proj",
    )(h, g, w)


def _sb_attn_kernel(q_ref, kT_ref, v_ref, o_ref, za_ref, zb_ref, aa_ref, ab_ref,
                    acc_ref, after_ref, later_ref, *, TK, R):
    i = pl.program_id(1)
    row = lax.broadcasted_iota(jnp.int32, (TK, TK), 0)
    col = lax.broadcasted_iota(jnp.int32, (TK, TK), 1)
    later_ref[...] = (row > col).astype(F32).astype(BF16)
    last = R * i + R - 1

    def key_start(t):
        return pl.multiple_of(jnp.maximum(last - t, 0) * TK, TK)

    def logits(t, z_ref):
        z_ref[...] = jnp.dot(q_ref[...], kT_ref[0, :, pl.ds(key_start(t), TK)],
                             preferred_element_type=F32)

    def pending_av(t, a_ref):
        acc_ref[...] += jnp.dot(a_ref[...], v_ref[pl.ds(key_start(jnp.maximum(t, 0)), TK), :],
                                preferred_element_type=F32)

    def weights(z_ref, a_ref, mask):
        for r in range(R):
            rows = slice(r * TK, (r + 1) * TK)
            kind = None if mask is None else mask[r]
            if kind == "none":
                a_ref[rows, :] = jnp.zeros((TK, TK), BF16)
                continue
            z = z_ref[rows, :]
            sp = jnp.maximum(z, 0.0) + jnp.log2(1.0 + jnp.exp2(_neg_abs(z)))
            if kind == "diag":
                valid = col < row
                sp = jnp.where(valid, sp, 0.0)
            within = jnp.dot(sp.astype(BF16), later_ref[...], preferred_element_type=F32)
            after = after_ref[rows, :]
            a = jnp.exp2(z - sp - within - after)
            if kind == "diag":
                a = jnp.where(valid, a, 0.0)
            a_ref[rows, :] = a.astype(BF16)
            after_ref[rows, :] = after + jnp.sum(sp, axis=1, keepdims=True)

    def pair(t, masks):
        logits(t + 1, zb_ref)
        pending_av(t - 1, ab_ref)
        weights(za_ref, aa_ref, masks[0])
        logits(t + 2, za_ref)
        pending_av(t, aa_ref)
        weights(zb_ref, ab_ref, masks[1])

    acc_ref[...] = jnp.zeros(acc_ref.shape, F32)
    after_ref[...] = jnp.zeros(after_ref.shape, F32)
    ab_ref[...] = jnp.zeros(ab_ref.shape, BF16)
    logits(0, za_ref)

    def diagonal_mask(t):
        beside = R - 1 - t
        return tuple("diag" if r == beside else ("none" if r < beside else None) for r in range(R))

    for t in range(0, R, 2):
        pair(t, (diagonal_mask(t), diagonal_mask(t + 1)))

    def body(j, carry):
        for t in range(0, R, 2):
            pair(R * j + R + t, (None, None))
        return carry

    lax.fori_loop(0, i, body, 0)
    pending_av(last, ab_ref)
    o_ref[...] = acc_ref[...].astype(BF16)


def _sb_attn(q, kT, v, tk=256, r=4):
    S = q.shape[0]
    tk = min(tk, S // r)
    tq = r * tk
    Dh = SB_HEAD_DIM
    return pl.pallas_call(
        functools.partial(_sb_attn_kernel, TK=tk, R=r),
        grid=(SB_HEADS, S // tq),
        in_specs=[
            pl.BlockSpec((tq, Dh), lambda h, i: (i, h)),
            pl.BlockSpec((1, Dh, S), lambda h, i: (h, 0, 0)),
            pl.BlockSpec((S, Dh), lambda h, i: (0, h)),
        ],
        out_specs=pl.BlockSpec((tq, Dh), lambda h, i: (i, h)),
        out_shape=jax.ShapeDtypeStruct((S, SB_HEADS * Dh), BF16),
        scratch_shapes=[
            pltpu.VMEM((tq, tk), F32),
            pltpu.VMEM((tq, tk), F32),
            pltpu.VMEM((tq, tk), BF16),
            pltpu.VMEM((tq, tk), BF16),
            pltpu.VMEM((tq, Dh), F32),
            pltpu.VMEM((tq, 1), F32),
            pltpu.VMEM((tk, tk), BF16),
        ],
        compiler_params=_params(("parallel", "arbitrary"), 48),
        name="sb_attn",
    )(q, kT, v)


def _out_proj_kernel(o_ref, h_ref, w_ref, g_ref, h1_ref, xn_ref):
    h1 = h_ref[...] + jnp.dot(o_ref[...], w_ref[...], preferred_element_type=F32)
    h1_ref[...] = h1
    xn_ref[...] = _rms_rows(h1, g_ref[...]).astype(BF16)


def _out_proj(o, h, layer, w, g, tm=256):
    S, D = h.shape
    tm = min(tm, S)
    return pl.pallas_call(
        _out_proj_kernel,
        grid=(S // tm,),
        in_specs=[
            pl.BlockSpec((tm, o.shape[1]), lambda i: (i, 0)),
            pl.BlockSpec((tm, D), lambda i: (i, 0)),
            _resident_layer(w, layer),
            _resident((1, D)),
        ],
        out_specs=[
            pl.BlockSpec((tm, D), lambda i: (i, 0)),
            pl.BlockSpec((tm, D), lambda i: (i, 0)),
        ],
        out_shape=[
            jax.ShapeDtypeStruct((S, D), F32),
            jax.ShapeDtypeStruct((S, D), BF16),
        ],
        compiler_params=_params(("parallel",), 48),
        name="out_proj",
    )(o, h, w, g)


def _mlp_up_kernel(x_ref, w_ref, wd_ref, wg_ref, u_ref, wd_bf16_ref, wg_bf16_ref):
    u = jnp.dot(x_ref[...], w_ref[...].astype(BF16), preferred_element_type=F32)
    u = jnp.maximum(u, 0.0)
    u_ref[...] = (u * u).astype(BF16)
    wd_bf16_ref[...] = wd_ref[...].astype(BF16)
    wg_bf16_ref[...] = wg_ref[...].astype(BF16)


def _mlp_up(xn, layer, w, w_down, w_gate, tm=2048, tn=512):
    S, D = xn.shape
    F = w.shape[2]
    tm, tn = min(tm, S), min(tn, F)
    ni, nj = S // tm, F // tn
    steps = ni * nj
    rows_d, rows_g = F // steps, D // steps
    assert rows_d * steps == F and rows_g * steps == D and rows_g % 16 == 0
    slab = lambda i, j: i * nj + j
    return pl.pallas_call(
        _mlp_up_kernel,
        grid=(ni, nj),
        in_specs=[
            pl.BlockSpec((tm, D), lambda i, j: (i, 0), pipeline_mode=pl.Buffered(1)),
            pl.BlockSpec((None, D, tn), lambda i, j: (layer, 0, j)),
            pl.BlockSpec((None, rows_d, D), lambda i, j: (layer, slab(i, j), 0)),
            pl.BlockSpec((None, rows_g, D), lambda i, j: (layer, slab(i, j), 0)),
        ],
        out_specs=[
            pl.BlockSpec((tm, tn), lambda i, j: (i, j)),
            pl.BlockSpec((rows_d, D), lambda i, j: (slab(i, j), 0)),
            pl.BlockSpec((rows_g, D), lambda i, j: (slab(i, j), 0)),
        ],
        out_shape=[
            jax.ShapeDtypeStruct((S, F), BF16),
            jax.ShapeDtypeStruct((F, D), BF16),
            jax.ShapeDtypeStruct((D, D), BF16),
        ],
        compiler_params=_params(("parallel", "arbitrary"), 52),
        name="mlp_up",
    )(xn, w, w_down, w_gate)


def _mlp_down_kernel(u_ref, w_ref, r_ref, g_ref, o_ref, hg_ref, ss_ref):
    k = pl.program_id(2)

    @pl.when(k == 0)
    def _():
        o_ref[...] = r_ref[...]

    o_ref[...] += jnp.dot(u_ref[...], w_ref[...], preferred_element_type=F32)

    @pl.when(k == pl.num_programs(2) - 1)
    def _():
        o = o_ref[...]
        hg_ref[...] = (o * g_ref[...]).astype(BF16)
        ss_ref[...] = jnp.broadcast_to(jnp.sum(o * o, axis=-1, keepdims=True), ss_ref.shape)


def _mlp_down(u, w, res, g, tm=1024, tn=1024, tk=2048):
    S, F = u.shape
    D = w.shape[1]
    tm, tn, tk = min(tm, S), min(tn, D), min(tk, F)
    return pl.pallas_call(
        _mlp_down_kernel,
        grid=(S // tm, D // tn, F // tk),
        in_specs=[
            pl.BlockSpec((tm, tk), lambda i, j, k: (i, k)),
            pl.BlockSpec((tk, tn), lambda i, j, k: (k, j)),
            pl.BlockSpec((tm, tn), lambda i, j, k: (i, j)),
            pl.BlockSpec((1, tn), lambda i, j, k: (0, j)),
        ],
        out_specs=[
            pl.BlockSpec((tm, tn), lambda i, j, k: (i, j)),
            pl.BlockSpec((tm, tn), lambda i, j, k: (i, j)),
            pl.BlockSpec((tm, LANES), lambda i, j, k: (i, j)),
        ],
        out_shape=[
            jax.ShapeDtypeStruct((S, D), F32),
            jax.ShapeDtypeStruct((S, D), BF16),
            jax.ShapeDtypeStruct((S, (D // tn) * LANES), F32),
        ],
        compiler_params=_params(("parallel", "parallel", "arbitrary"), 48),
        name="mlp_down",
    )(u, w, res, g)


PLE_ROW_GROUPS = 4


def _ple_kernel(hg_ref, ss_ref, h_ref, wg_ref, p_ref, wp_ref, o_ref, *, d_model):
    rows_per_group = hg_ref.shape[0] // PLE_ROW_GROUPS
    for r in range(PLE_ROW_GROUPS):
        rows = slice(r * rows_per_group, (r + 1) * rows_per_group)
        ss = ss_ref[rows, :]
        total = ss[:, 0:1]
        for t in range(1, ss.shape[1] // LANES):
            total = total + ss[:, t * LANES:t * LANES + 1]
        inv = lax.rsqrt(total * (1.0 / d_model) + EPS)
        logits = jnp.dot(hg_ref[rows, :], wg_ref[...], preferred_element_type=F32) * inv
        gate = 1.0 / (1.0 + jnp.exp(-logits))
        emb = jnp.dot(p_ref[rows, :].astype(BF16), wp_ref[...], preferred_element_type=F32)
        o_ref[rows, :] = h_ref[rows, :] + emb * gate


def _ple(h, hg, ss, layer, wg, p, wp, tm=1024, tn=512):
    S, D = h.shape
    P = p.shape[-1]
    tm, tn = min(tm, S), min(tn, D)
    return pl.pallas_call(
        functools.partial(_ple_kernel, d_model=D),
        grid=(S // tm, D // tn),
        in_specs=[
            pl.BlockSpec((tm, D), lambda i, j: (i, 0)),
            pl.BlockSpec((tm, ss.shape[1]), lambda i, j: (i, 0)),
            pl.BlockSpec((tm, tn), lambda i, j: (i, j)),
            pl.BlockSpec((D, tn), lambda i, j: (0, j)),
            pl.BlockSpec((None, None, tm, P), lambda i, j: (layer, 0, i, 0)),
            pl.BlockSpec((None, P, tn), lambda i, j: (layer, 0, j)),
        ],
        out_specs=pl.BlockSpec((tm, tn), lambda i, j: (i, j)),
        out_shape=jax.ShapeDtypeStruct((S, D), F32),
        compiler_params=_params(("parallel", "arbitrary"), 48),
        name="ple_gate",
    )(hg, ss, h, wg, p, wp)


def _rope_tables(positions):
    inv_freq = ROPE_THETA ** (-jnp.arange(0, MLA_ROPE, 2, dtype=F32) / MLA_ROPE)
    ang = positions.astype(F32)[:, None] * inv_freq
    return jnp.cos(ang).T, jnp.sin(ang).T


def kernel(x, p, positions, norm_mix, norm_mlp, norm_ple, mla_w_in, mla_q_norm, mla_kv_norm, mla_w_uq, mla_w_ukv, mla_q_gain, mla_k_gain, mla_w_o, sb_w_qkv, sb_w_o, mlp_w_up, mlp_w_down, ple_w_proj, ple_w_gate):
    assert x.shape[0] == 1, "one sequence per call"
    depth = p.shape[0]
    cosT, sinT = _rope_tables(positions[0])
    col = lambda v: v.astype(F32)[:, None]
    row = lambda v: v.astype(F32)[None, :]
    transposed = lambda w: jnp.swapaxes(w, 1, 2).astype(BF16)
    mla_in_t, mla_uq_t, mla_ukv_t = transposed(mla_w_in), transposed(mla_w_uq), transposed(mla_w_ukv)
    mla_o, sb_qkv, sb_o = mla_w_o.astype(BF16), sb_w_qkv.astype(BF16), sb_w_o.astype(BF16)
    w_proj = ple_w_proj.astype(BF16)
    h = x[0]
    for i in range(depth):
        j = i // N_MIXERS
        if i % N_MIXERS == 0:
            qT, k, vT = _mla_proj(
                h, row(norm_mix[i]), j, mla_in_t, col(mla_q_norm[j]), col(mla_kv_norm[j]),
                mla_uq_t, mla_ukv_t, col(mla_q_gain[j]), col(mla_k_gain[j]), cosT, sinT)
            o = _mla_attn(qT, k, vT)
            w_o = mla_o
        else:
            q, kT, v = _sb_proj(h, row(norm_mix[i]), j, sb_qkv)
            o = _sb_attn(q, kT, v)
            w_o = sb_o
        h1, xn = _out_proj(o, h, j, w_o, row(norm_mlp[i]))
        u, w_down, w_gate = _mlp_up(xn, i, mlp_w_up, mlp_w_down, ple_w_gate)
        h2, hg, ss = _mlp_down(u, w_down, h1, row(norm_ple[i]))
        h = _ple(h2, hg, ss, i, w_gate, p, w_proj)
    return h[None]
```

```python
import functools

import jax
import jax.numpy as jnp
from jax import lax
from jax.experimental import pallas as pl
from jax.experimental.pallas import tpu as pltpu

F32 = jnp.float32
BF16 = jnp.bfloat16

CHUNK = 64
N_MIXERS = 2
MLA_HEADS = 8
MLA_Q_LORA = 1024
MLA_KV_LORA = 512
MLA_NOPE = 128
MLA_ROPE = 64
MLA_QK = MLA_NOPE + MLA_ROPE
MLA_QK_PAD = 256
MLA_V = 128
MLA_V_AUG = MLA_V + 16
ROPE_THETA = 10000.0
LOG2E = 1.4426950408889634
SB_HEADS = 4
SB_HEAD_DIM = 256
EPS = 1e-6

MIB = 1024 * 1024
V7X_VMEM_BYTES = 64 * MIB
LANES = 128


def _params(semantics, vmem_mib):
    assert vmem_mib * MIB < V7X_VMEM_BYTES
    return pltpu.CompilerParams(dimension_semantics=semantics, vmem_limit_bytes=vmem_mib * MIB)


def _resident(shape):
    return pl.BlockSpec(shape, lambda *_: (0,) * len(shape), pipeline_mode=pl.Buffered(1))


def _resident_layer(stack, layer):
    tail = stack.shape[1:]
    return pl.BlockSpec((None,) + tail, lambda *_: (layer,) + (0,) * len(tail),
                        pipeline_mode=pl.Buffered(1))


def _rms_rows(x, gain):
    return x * lax.rsqrt(jnp.mean(x * x, axis=-1, keepdims=True) + EPS) * gain


def _neg_abs(x):
    bits = lax.bitcast_convert_type(x, jnp.uint32) | jnp.uint32(0x80000000)
    return lax.bitcast_convert_type(bits, F32)


def _rms_cols(x, gain):
    return x * lax.rsqrt(jnp.mean(x * x, axis=0, keepdims=True) + EPS) * gain


def _mla_proj_kernel(h_ref, g_ref, w_in_ref, qn_ref, kvn_ref, w_uq_ref, w_ukv_ref,
                     qg_ref, kg_ref, cos_ref, sin_ref, qT_ref, k_ref, vT_ref,
                     cq_ref, ckv_ref, kr_ref):
    tm = h_ref.shape[0]
    tok = slice(0, tm)

    @pl.when(pl.program_id(0) == 0)
    def _():
        cq_ref[...] = jnp.zeros(cq_ref.shape, BF16)
        ckv_ref[...] = jnp.zeros(ckv_ref.shape, BF16)
        kr_ref[...] = jnp.zeros(kr_ref.shape, F32)

    _mla_heads(tok, cq_ref, ckv_ref, kr_ref, w_uq_ref, w_ukv_ref, qg_ref, kg_ref, cos_ref, sin_ref,
               qT_ref, k_ref, vT_ref)

    y = _rms_rows(h_ref[tok, :], g_ref[...]).astype(BF16)
    projT = lax.dot_general(w_in_ref[...], y, (((1,), (1,)), ((), ())),
                            preferred_element_type=F32)
    cq_ref[...] = _rms_cols(projT[:MLA_Q_LORA], qn_ref[...]).astype(BF16)
    ckv_ref[...] = _rms_cols(projT[MLA_Q_LORA:MLA_Q_LORA + MLA_KV_LORA], kvn_ref[...]).astype(BF16)
    kr_ref[...] = projT[MLA_Q_LORA + MLA_KV_LORA:]


def _mla_heads(tok, cq_ref, ckv_ref, kr_ref, w_uq_ref, w_ukv_ref, qg_ref, kg_ref, cos_ref, sin_ref,
               qT_ref, k_ref, vT_ref):
    tm = tok.stop - tok.start
    kr = kr_ref[...]
    qT = jnp.dot(w_uq_ref[...], cq_ref[...], preferred_element_type=F32)
    kvT = jnp.dot(w_ukv_ref[...], ckv_ref[...], preferred_element_type=F32)
    cos = cos_ref[:, tok]
    sin = sin_ref[:, tok]
    qg = qg_ref[...]
    kg = kg_ref[...]
    kr_ss = jnp.sum(kr * kr, axis=0, keepdims=True)
    scale = MLA_QK ** -0.5 * LOG2E
    half = MLA_ROPE // 2
    pad = MLA_QK_PAD - MLA_QK
    aug = MLA_V_AUG - MLA_V
    ones_tile = (lax.broadcasted_iota(jnp.int32, (aug, tm), 0) == 0).astype(F32).astype(BF16)
    for hd in range(MLA_HEADS):
        qh = _rms_cols(qT[hd * MLA_QK:(hd + 1) * MLA_QK], qg) * scale
        x1 = qh[MLA_NOPE:MLA_NOPE + half]
        x2 = qh[MLA_NOPE + half:]
        qT_ref[hd, 0:MLA_NOPE, tok] = qh[:MLA_NOPE].astype(BF16)
        qT_ref[hd, MLA_NOPE:MLA_NOPE + half, tok] = (x1 * cos - x2 * sin).astype(BF16)
        qT_ref[hd, MLA_NOPE + half:MLA_QK, tok] = (x2 * cos + x1 * sin).astype(BF16)
        qT_ref[hd, MLA_QK:, tok] = jnp.zeros((pad, tm), BF16)

        base = hd * (MLA_NOPE + MLA_V)
        kn = kvT[base:base + MLA_NOPE]
        inv = lax.rsqrt((jnp.sum(kn * kn, axis=0, keepdims=True) + kr_ss) * (1.0 / MLA_QK) + EPS)
        kn = kn * inv * kg[:MLA_NOPE]
        krh = kr * inv * kg[MLA_NOPE:]
        k1 = krh[:half]
        k2 = krh[half:]
        kT = jnp.concatenate(
            [kn, k1 * cos - k2 * sin, k2 * cos + k1 * sin, jnp.zeros((pad, tm), F32)], axis=0)
        k_ref[hd, tok, :] = kT.T.astype(BF16)
        vT_ref[hd, :MLA_V, tok] = kvT[base + MLA_NOPE:base + MLA_NOPE + MLA_V].astype(BF16)
        vT_ref[hd, MLA_V:, tok] = ones_tile


def _mla_proj(h, g, layer, w_inT, qn, kvn, w_uqT, w_ukvT, qg, kg, cosT, sinT, tm=256):
    S, D = h.shape
    tm = min(tm, S)
    H = MLA_HEADS
    n = S // tm
    latents = lambda s: jnp.minimum(s, n - 1)
    heads = lambda s: jnp.maximum(s - 1, 0)
    return pl.pallas_call(
        _mla_proj_kernel,
        grid=(n + 1,),
        in_specs=[
            pl.BlockSpec((tm, D), lambda s: (latents(s), 0)),
            _resident((1, D)),
            _resident_layer(w_inT, layer),
            _resident(qn.shape),
            _resident(kvn.shape),
            _resident_layer(w_uqT, layer),
            _resident_layer(w_ukvT, layer),
            _resident(qg.shape),
            _resident(kg.shape),
            pl.BlockSpec((MLA_ROPE // 2, tm), lambda s: (0, heads(s))),
            pl.BlockSpec((MLA_ROPE // 2, tm), lambda s: (0, heads(s))),
        ],
        out_specs=[
            pl.BlockSpec((H, MLA_QK_PAD, tm), lambda s: (0, 0, heads(s))),
            pl.BlockSpec((H, tm, MLA_QK_PAD), lambda s: (0, heads(s), 0)),
            pl.BlockSpec((H, MLA_V_AUG, tm), lambda s: (0, 0, heads(s))),
        ],
        out_shape=[
            jax.ShapeDtypeStruct((H, MLA_QK_PAD, S), BF16),
            jax.ShapeDtypeStruct((H, S, MLA_QK_PAD), BF16),
            jax.ShapeDtypeStruct((H, MLA_V_AUG, S), BF16),
        ],
        scratch_shapes=[
            pltpu.VMEM((MLA_Q_LORA, tm), BF16),
            pltpu.VMEM((MLA_KV_LORA, tm), BF16),
            pltpu.VMEM((MLA_ROPE, tm), F32),
        ],
        compiler_params=_params(("arbitrary",), 52),
        name="mla_proj",
    )(h, g, w_inT, qn, kvn, w_uqT, w_ukvT, qg, kg, cosT, sinT)


MLA_DEPTH = 4


def _mla_attn_kernel(qT_ref, k_ref, vT_ref, o_ref, s0_ref, s1_ref, s2_ref, s3_ref, p0_ref, p1_ref,
                     m_ref, alpha_ref, acc_ref, *, TK):
    i = pl.program_id(1)
    N = MLA_DEPTH
    TQ = N * TK
    s_refs = (s0_ref, s1_ref, s2_ref, s3_ref)
    p_refs = (p0_ref, p1_ref)

    def lo(u):
        return u * TK if 0 <= u < N else 0

    def scores(block, u):
        start = pl.multiple_of(block * TK, TK)
        s_refs[u % N][:, lo(u):] = jnp.dot(k_ref[0, pl.ds(start, TK), :], qT_ref[0, :, lo(u):],
                                           preferred_element_type=F32)

    def pv(block, u, trimmed=True):
        c = lo(u) if trimmed else 0
        start = pl.multiple_of(block * TK, TK)
        prod = jnp.dot(vT_ref[0, :, pl.ds(start, TK)], p_refs[u % 2][:, c:],
                       preferred_element_type=F32)
        acc_ref[:, c:] = alpha_ref[u % N, :, c:] * acc_ref[:, c:] + prod

    def running_max(u, diagonal):
        s_ref = s_refs[u % N]
        c = lo(u)
        if diagonal:
            kc = lax.broadcasted_iota(jnp.int32, (TK, TK), 0) // CHUNK
            qc = lax.broadcasted_iota(jnp.int32, (TK, TK), 1) // CHUNK
            s_ref[:, c:c + TK] = jnp.where(kc <= qc, s_ref[:, c:c + TK], -jnp.inf)
        m_prev = m_ref[(u - 1) % N, :, c:]
        m_new = jnp.maximum(m_prev, jnp.max(s_ref[:, c:], axis=0, keepdims=True))
        m_ref[u % N, :, c:] = m_new
        alpha_ref[u % N, :, c:] = jnp.exp2(m_prev - m_new)
        if c:
            m_ref[u % N, :, :c] = m_ref[(u - 1) % N, :, :c]
            alpha_ref[u % N, :, :c] = jnp.ones((1, c), F32)

    def probabilities(u, zero_hidden=False):
        c = lo(u)
        p_refs[u % 2][:, c:] = jnp.exp2(s_refs[u % N][:, c:] - m_ref[u % N, :, c:]).astype(BF16)
        if zero_hidden and c:
            p_refs[u % 2][:, :c] = jnp.zeros((TK, c), BF16)

    def step(u, block_ahead, block_behind, diagonal_next=False, last_diagonal=False):
        scores(block_ahead, u + 2)
        pv(block_behind, u - 1, trimmed=not (u - 1 == N - 1))
        probabilities(u, zero_hidden=last_diagonal)
        running_max(u + 1, diagonal_next)

    first = N * i
    acc_ref[...] = jnp.zeros(acc_ref.shape, F32)
    p_refs[1][...] = jnp.zeros(p_refs[1].shape, BF16)
    alpha_ref[N - 1] = jnp.ones(alpha_ref.shape[1:], F32)
    m_ref[N - 1] = jnp.full(m_ref.shape[1:], -jnp.inf, F32)
    scores(first, 0)
    scores(first + 1, 1)
    running_max(0, True)

    step(0, first + 2, first, diagonal_next=True)
    step(1, first + 3, first, diagonal_next=True)
    step(2, 0, first + 1, diagonal_next=True)
    step(3, 1, first + 2, last_diagonal=True)

    def four_steps(j):
        base = N * j
        step(N, base + 2, jnp.where(j == 0, first + 3, base - 1))
        step(N + 1, base + 3, base)
        step(N + 2, base + 4, base + 1)
        step(N + 3, base + 5, base + 2)

    def body(jj, carry):
        four_steps(2 * jj)
        four_steps(2 * jj + 1)
        return carry

    lax.fori_loop(0, i // 2, body, 0)

    @pl.when(i % 2 == 1)
    def _():
        four_steps(i - 1)
    pv(jnp.where(i == 0, first + 3, first - 1), N - 1, trimmed=False)
    o = acc_ref[:MLA_V, :] / acc_ref[MLA_V:MLA_V + 1, :]
    o_ref[...] = o.T.astype(BF16)


def _mla_attn(qT, k, vT, tk=512):
    H, _, S = qT.shape
    tk = min(tk, S // MLA_DEPTH)
    tq = MLA_DEPTH * tk
    return pl.pallas_call(
        functools.partial(_mla_attn_kernel, TK=tk),
        grid=(H, S // tq),
        in_specs=[
            pl.BlockSpec((1, MLA_QK_PAD, tq), lambda h, i: (h, 0, i)),
            pl.BlockSpec((1, S, MLA_QK_PAD), lambda h, i: (h, 0, 0)),
            pl.BlockSpec((1, MLA_V_AUG, S), lambda h, i: (h, 0, 0)),
        ],
        out_specs=pl.BlockSpec((tq, MLA_V), lambda h, i: (i, h)),
        out_shape=jax.ShapeDtypeStruct((S, H * MLA_V), BF16),
        scratch_shapes=(
            [pltpu.VMEM((tk, tq), F32)] * MLA_DEPTH
            + [pltpu.VMEM((tk, tq), BF16)] * 2
            + [pltpu.VMEM((MLA_DEPTH, 1, tq), F32),
               pltpu.VMEM((MLA_DEPTH, 1, tq), F32),
               pltpu.VMEM((MLA_V_AUG, tq), F32)]
        ),
        compiler_params=_params(("parallel", "arbitrary"), 54),
        name="mla_attn",
    )(qT, k, vT)


def _sb_proj_kernel(h_ref, g_ref, w_ref, q_ref, kT_ref, v_ref):
    HD = SB_HEADS * SB_HEAD_DIM
    y = _rms_rows(h_ref[...], g_ref[...]).astype(BF16)
    qkv = jnp.dot(y, w_ref[...], preferred_element_type=F32)
    q_ref[...] = (qkv[:, :HD] * (SB_HEAD_DIM ** -0.5 * LOG2E)).astype(BF16)
    v_ref[...] = qkv[:, 2 * HD:].astype(BF16)
    for hd in range(SB_HEADS):
        k = qkv[:, HD + hd * SB_HEAD_DIM:HD + (hd + 1) * SB_HEAD_DIM]
        kT_ref[hd] = k.T.astype(BF16)


def _sb_proj(h, g, layer, w, tm=256):
    S, D = h.shape
    tm = min(tm, S)
    HD = SB_HEADS * SB_HEAD_DIM
    return pl.pallas_call(
        _sb_proj_kernel,
        grid=(S // tm,),
        in_specs=[
            pl.BlockSpec((tm, D), lambda i: (i, 0)),
            _resident((1, D)),
            _resident_layer(w, layer),
        ],
        out_specs=[
            pl.BlockSpec((tm, HD), lambda i: (i, 0)),
            pl.BlockSpec((SB_HEADS, SB_HEAD_DIM, tm), lambda i: (0, 0, i)),
            pl.BlockSpec((tm, HD), lambda i: (i, 0)),
        ],
        out_shape=[
            jax.ShapeDtypeStruct((S, HD), BF16),
            jax.ShapeDtypeStruct((SB_HEADS, SB_HEAD_DIM, S), BF16),
            jax.ShapeDtypeStruct((S, HD), BF16),
        ],
        compiler_params=_params(("parallel",), 52),
        name="sb_proj",
    )(h, g, w)


def _sb_attn_kernel(q_ref, kT_ref, v_ref, o_ref, za_ref, zb_ref, aa_ref, ab_ref,
                    acc_ref, after_ref, later_ref, *, TK, R):
    i = pl.program_id(1)
    row = lax.broadcasted_iota(jnp.int32, (TK, TK), 0)
    col = lax.broadcasted_iota(jnp.int32, (TK, TK), 1)
    later_ref[...] = (row > col).astype(F32).astype(BF16)
    last = R * i + R - 1

    def key_start(t):
        return pl.multiple_of(jnp.maximum(last - t, 0) * TK, TK)

    def logits(t, z_ref):
        z_ref[...] = jnp.dot(q_ref[...], kT_ref[0, :, pl.ds(key_start(t), TK)],
                             preferred_element_type=F32)

    def pending_av(t, a_ref):
        acc_ref[...] += jnp.dot(a_ref[...], v_ref[pl.ds(key_start(jnp.maximum(t, 0)), TK), :],
                                preferred_element_type=F32)

    def weights(z_ref, a_ref, mask):
        for r in range(R):
            rows = slice(r * TK, (r + 1) * TK)
            kind = None if mask is None else mask[r]
            if kind == "none":
                a_ref[rows, :] = jnp.zeros((TK, TK), BF16)
                continue
            z = z_ref[rows, :]
            sp = jnp.maximum(z, 0.0) + jnp.log2(1.0 + jnp.exp2(_neg_abs(z)))
            if kind == "diag":
                valid = col < row
                sp = jnp.where(valid, sp, 0.0)
            within = jnp.dot(sp.astype(BF16), later_ref[...], preferred_element_type=F32)
            after = after_ref[rows, :]
            a = jnp.exp2(z - sp - within - after)
            if kind == "diag":
                a = jnp.where(valid, a, 0.0)
            a_ref[rows, :] = a.astype(BF16)
            after_ref[rows, :] = after + jnp.sum(sp, axis=1, keepdims=True)

    def pair(t, masks):
        logits(t + 1, zb_ref)
        pending_av(t - 1, ab_ref)
        weights(za_ref, aa_ref, masks[0])
        logits(t + 2, za_ref)
        pending_av(t, aa_ref)
        weights(zb_ref, ab_ref, masks[1])

    acc_ref[...] = jnp.zeros(acc_ref.shape, F32)
    after_ref[...] = jnp.zeros(after_ref.shape, F32)
    ab_ref[...] = jnp.zeros(ab_ref.shape, BF16)
    logits(0, za_ref)

    def diagonal_mask(t):
        beside = R - 1 - t
        return tuple("diag" if r == beside else ("none" if r < beside else None) for r in range(R))

    for t in range(0, R, 2):
        pair(t, (diagonal_mask(t), diagonal_mask(t + 1)))

    def body(j, carry):
        for t in range(0, R, 2):
            pair(R * j + R + t, (None, None))
        return carry

    lax.fori_loop(0, i, body, 0)
    pending_av(last, ab_ref)
    o_ref[...] = acc_ref[...].astype(BF16)


def _sb_attn(q, kT, v, tk=256, r=4):
    S = q.shape[0]
    tk = min(tk, S // r)
    tq = r * tk
    Dh = SB_HEAD_DIM
    return pl.pallas_call(
        functools.partial(_sb_attn_kernel, TK=tk, R=r),
        grid=(SB_HEADS, S // tq),
        in_specs=[
            pl.BlockSpec((tq, Dh), lambda h, i: (i, h)),
            pl.BlockSpec((1, Dh, S), lambda h, i: (h, 0, 0)),
            pl.BlockSpec((S, Dh), lambda h, i: (0, h)),
        ],
        out_specs=pl.BlockSpec((tq, Dh), lambda h, i: (i, h)),
        out_shape=jax.ShapeDtypeStruct((S, SB_HEADS * Dh), BF16),
        scratch_shapes=[
            pltpu.VMEM((tq, tk), F32),
            pltpu.VMEM((tq, tk), F32),
            pltpu.VMEM((tq, tk), BF16),
            pltpu.VMEM((tq, tk), BF16),
            pltpu.VMEM((tq, Dh), F32),
            pltpu.VMEM((tq, 1), F32),
            pltpu.VMEM((tk, tk), BF16),
        ],
        compiler_params=_params(("parallel", "arbitrary"), 48),
        name="sb_attn",
    )(q, kT, v)


def _out_proj_kernel(o_ref, h_ref, w_ref, g_ref, h1_ref, xn_ref):
    h1 = h_ref[...] + jnp.dot(o_ref[...], w_ref[...], preferred_element_type=F32)
    h1_ref[...] = h1
    xn_ref[...] = _rms_rows(h1, g_ref[...]).astype(BF16)


def _out_proj(o, h, layer, w, g, tm=256):
    S, D = h.shape
    tm = min(tm, S)
    return pl.pallas_call(
        _out_proj_kernel,
        grid=(S // tm,),
        in_specs=[
            pl.BlockSpec((tm, o.shape[1]), lambda i: (i, 0)),
            pl.BlockSpec((tm, D), lambda i: (i, 0)),
            _resident_layer(w, layer),
            _resident((1, D)),
        ],
        out_specs=[
            pl.BlockSpec((tm, D), lambda i: (i, 0)),
            pl.BlockSpec((tm, D), lambda i: (i, 0)),
        ],
        out_shape=[
            jax.ShapeDtypeStruct((S, D), F32),
            jax.ShapeDtypeStruct((S, D), BF16),
        ],
        compiler_params=_params(("parallel",), 48),
        name="out_proj",
    )(o, h, w, g)


def _mlp_up_kernel(x_ref, w_ref, wd_ref, wg_ref, u_ref, wd_bf16_ref, wg_bf16_ref):
    u = jnp.dot(x_ref[...], w_ref[...].astype(BF16), preferred_element_type=F32)
    u = jnp.maximum(u, 0.0)
    u_ref[...] = (u * u).astype(BF16)
    wd_bf16_ref[...] = wd_ref[...].astype(BF16)
    wg_bf16_ref[...] = wg_ref[...].astype(BF16)


def _mlp_up(xn, layer, w, w_down, w_gate, tm=2048, tn=512):
    S, D = xn.shape
    F = w.shape[2]
    tm, tn = min(tm, S), min(tn, F)
    ni, nj = S // tm, F // tn
    steps = ni * nj
    rows_d, rows_g = F // steps, D // steps
    assert rows_d * steps == F and rows_g * steps == D and rows_g % 16 == 0
    slab = lambda i, j: i * nj + j
    return pl.pallas_call(
        _mlp_up_kernel,
        grid=(ni, nj),
        in_specs=[
            pl.BlockSpec((tm, D), lambda i, j: (i, 0), pipeline_mode=pl.Buffered(1)),
            pl.BlockSpec((None, D, tn), lambda i, j: (layer, 0, j)),
            pl.BlockSpec((None, rows_d, D), lambda i, j: (layer, slab(i, j), 0)),
            pl.BlockSpec((None, rows_g, D), lambda i, j: (layer, slab(i, j), 0)),
        ],
        out_specs=[
            pl.BlockSpec((tm, tn), lambda i, j: (i, j)),
            pl.BlockSpec((rows_d, D), lambda i, j: (slab(i, j), 0)),
            pl.BlockSpec((rows_g, D), lambda i, j: (slab(i, j), 0)),
        ],
        out_shape=[
            jax.ShapeDtypeStruct((S, F), BF16),
            jax.ShapeDtypeStruct((F, D), BF16),
            jax.ShapeDtypeStruct((D, D), BF16),
        ],
        compiler_params=_params(("parallel", "arbitrary"), 52),
        name="mlp_up",
    )(xn, w, w_down, w_gate)


def _mlp_down_kernel(u_ref, w_ref, res_hbm, g_ref, o_ref, hg_ref, ss_ref, res_buf, res_sem):
    i, j, k = pl.program_id(0), pl.program_id(1), pl.program_id(2)
    tm, tn = res_buf.shape

    def residual_copy():
        rows = pl.ds(pl.multiple_of(i * tm, tm), tm)
        cols = pl.ds(pl.multiple_of(j * tn, tn), tn)
        return pltpu.make_async_copy(res_hbm.at[rows, cols], res_buf, res_sem)

    @pl.when(k == 0)
    def _():
        residual_copy().start()
        o_ref[...] = jnp.zeros(o_ref.shape, F32)

    o_ref[...] += jnp.dot(u_ref[...], w_ref[...], preferred_element_type=F32)

    @pl.when(k == pl.num_programs(2) - 1)
    def _():
        residual_copy().wait()
        o = o_ref[...] + res_buf[...]
        o_ref[...] = o
        hg_ref[...] = (o * g_ref[...]).astype(BF16)
        ss_ref[...] = jnp.broadcast_to(jnp.sum(o * o, axis=-1, keepdims=True), ss_ref.shape)


def _mlp_down(u, w, res, g, tm=1024, tn=1024, tk=4096):
    S, F = u.shape
    D = w.shape[1]
    tm, tn, tk = min(tm, S), min(tn, D), min(tk, F)
    return pl.pallas_call(
        _mlp_down_kernel,
        grid=(S // tm, D // tn, F // tk),
        in_specs=[
            pl.BlockSpec((tm, tk), lambda i, j, k: (i, k)),
            pl.BlockSpec((tk, tn), lambda i, j, k: (k, j)),
            pl.BlockSpec(memory_space=pl.ANY),
            pl.BlockSpec((1, tn), lambda i, j, k: (0, j)),
        ],
        out_specs=[
            pl.BlockSpec((tm, tn), lambda i, j, k: (i, j)),
            pl.BlockSpec((tm, tn), lambda i, j, k: (i, j)),
            pl.BlockSpec((tm, LANES), lambda i, j, k: (i, j)),
        ],
        out_shape=[
            jax.ShapeDtypeStruct((S, D), F32),
            jax.ShapeDtypeStruct((S, D), BF16),
            jax.ShapeDtypeStruct((S, (D // tn) * LANES), F32),
        ],
        scratch_shapes=[pltpu.VMEM((tm, tn), F32), pltpu.SemaphoreType.DMA(())],
        compiler_params=_params(("arbitrary", "arbitrary", "arbitrary"), 58),
        name="mlp_down",
    )(u, w, res, g)


PLE_ROW_GROUPS = 4


def _ple_kernel(hg_ref, ss_ref, h_ref, wg_ref, p_ref, wp_ref, o_ref, *, d_model):
    rows_per_group = hg_ref.shape[0] // PLE_ROW_GROUPS
    for r in range(PLE_ROW_GROUPS):
        rows = slice(r * rows_per_group, (r + 1) * rows_per_group)
        ss = ss_ref[rows, :]
        total = ss[:, 0:1]
        for t in range(1, ss.shape[1] // LANES):
            total = total + ss[:, t * LANES:t * LANES + 1]
        inv = lax.rsqrt(total * (1.0 / d_model) + EPS)
        logits = jnp.dot(hg_ref[rows, :], wg_ref[...], preferred_element_type=F32) * inv
        gate = 1.0 / (1.0 + jnp.exp(-logits))
        emb = jnp.dot(p_ref[rows, :].astype(BF16), wp_ref[...], preferred_element_type=F32)
        o_ref[rows, :] = h_ref[rows, :] + emb * gate


def _ple(h, hg, ss, layer, wg, p, wp, tm=1024, tn=512):
    S, D = h.shape
    P = p.shape[-1]
    tm, tn = min(tm, S), min(tn, D)
    return pl.pallas_call(
        functools.partial(_ple_kernel, d_model=D),
        grid=(S // tm, D // tn),
        in_specs=[
            pl.BlockSpec((tm, D), lambda i, j: (i, 0)),
            pl.BlockSpec((tm, ss.shape[1]), lambda i, j: (i, 0)),
            pl.BlockSpec((tm, tn), lambda i, j: (i, j)),
            pl.BlockSpec((D, tn), lambda i, j: (0, j)),
            pl.BlockSpec((None, None, tm, P), lambda i, j: (layer, 0, i, 0)),
            pl.BlockSpec((None, P, tn), lambda i, j: (layer, 0, j)),
        ],
        out_specs=pl.BlockSpec((tm, tn), lambda i, j: (i, j)),
        out_shape=jax.ShapeDtypeStruct((S, D), F32),
        compiler_params=_params(("parallel", "arbitrary"), 48),
        name="ple_gate",
    )(hg, ss, h, wg, p, wp)


def _rope_tables(positions):
    inv_freq = ROPE_THETA ** (-jnp.arange(0, MLA_ROPE, 2, dtype=F32) / MLA_ROPE)
    ang = positions.astype(F32)[:, None] * inv_freq
    return jnp.cos(ang).T, jnp.sin(ang).T


def kernel(x, p, positions, norm_mix, norm_mlp, norm_ple, mla_w_in, mla_q_norm, mla_kv_norm, mla_w_uq, mla_w_ukv, mla_q_gain, mla_k_gain, mla_w_o, sb_w_qkv, sb_w_o, mlp_w_up, mlp_w_down, ple_w_proj, ple_w_gate):
    assert x.shape[0] == 1, "one sequence per call"
    depth = p.shape[0]
    cosT, sinT = _rope_tables(positions[0])
    col = lambda v: v.astype(F32)[:, None]
    row = lambda v: v.astype(F32)[None, :]
    transposed = lambda w: jnp.swapaxes(w, 1, 2).astype(BF16)
    mla_in_t, mla_uq_t, mla_ukv_t = transposed(mla_w_in), transposed(mla_w_uq), transposed(mla_w_ukv)
    mla_o, sb_qkv, sb_o = mla_w_o.astype(BF16), sb_w_qkv.astype(BF16), sb_w_o.astype(BF16)
    w_proj = ple_w_proj.astype(BF16)
    h = x[0]
    for i in range(depth):
        j = i // N_MIXERS
        if i % N_MIXERS == 0:
            qT, k, vT = _mla_proj(
                h, row(norm_mix[i]), j, mla_in_t, col(mla_q_norm[j]), col(mla_kv_norm[j]),
                mla_uq_t, mla_ukv_t, col(mla_q_gain[j]), col(mla_k_gain[j]), cosT, sinT)
            o = _mla_attn(qT, k, vT)
            w_o = mla_o
        else:
            q, kT, v = _sb_proj(h, row(norm_mix[i]), j, sb_qkv)
            o = _sb_attn(q, kT, v)
            w_o = sb_o
        h1, xn = _out_proj(o, h, j, w_o, row(norm_mlp[i]))
        u, w_down, w_gate = _mlp_up(xn, i, mlp_w_up, mlp_w_down, ple_w_gate)
        h2, hg, ss = _mlp_down(u, w_down, h1, row(norm_ple[i]))
        h = _ple(h2, hg, ss, i, w_gate, p, w_proj)
    return h[None]
```

```python
import functools

import jax
import jax.numpy as jnp
from jax import lax
from jax.experimental import pallas as pl
from jax.experimental.pallas import tpu as pltpu

F32 = jnp.float32
BF16 = jnp.bfloat16

CHUNK = 64
N_MIXERS = 2
MLA_HEADS = 8
MLA_Q_LORA = 1024
MLA_KV_LORA = 512
MLA_NOPE = 128
MLA_ROPE = 64
MLA_QK = MLA_NOPE + MLA_ROPE
MLA_QK_PAD = 256
MLA_V = 128
MLA_V_AUG = MLA_V + 16
ROPE_THETA = 10000.0
LOG2E = 1.4426950408889634
SB_HEADS = 4
SB_HEAD_DIM = 256
EPS = 1e-6

MIB = 1024 * 1024
V7X_VMEM_BYTES = 64 * MIB
LANES = 128


def _params(semantics, vmem_mib):
    assert vmem_mib * MIB < V7X_VMEM_BYTES
    return pltpu.CompilerParams(dimension_semantics=semantics, vmem_limit_bytes=vmem_mib * MIB)


def _resident(shape):
    return pl.BlockSpec(shape, lambda *_: (0,) * len(shape), pipeline_mode=pl.Buffered(1))


def _resident_layer(stack, layer):
    tail = stack.shape[1:]
    return pl.BlockSpec((None,) + tail, lambda *_: (layer,) + (0,) * len(tail),
                        pipeline_mode=pl.Buffered(1))


def _rms_rows(x, gain):
    return x * lax.rsqrt(jnp.mean(x * x, axis=-1, keepdims=True) + EPS) * gain


def _neg_abs(x):
    bits = lax.bitcast_convert_type(x, jnp.uint32) | jnp.uint32(0x80000000)
    return lax.bitcast_convert_type(bits, F32)


def _rms_cols(x, gain):
    return x * lax.rsqrt(jnp.mean(x * x, axis=0, keepdims=True) + EPS) * gain


def _mla_proj_kernel(h_ref, g_ref, w_in_ref, qn_ref, kvn_ref, w_uq_ref, w_ukv_ref,
                     qg_ref, kg_ref, cos_ref, sin_ref, qT_ref, k_ref, vT_ref,
                     cq_ref, ckv_ref, kr_ref):
    tm = h_ref.shape[0]
    tok = slice(0, tm)

    @pl.when(pl.program_id(0) == 0)
    def _():
        cq_ref[...] = jnp.zeros(cq_ref.shape, BF16)
        ckv_ref[...] = jnp.zeros(ckv_ref.shape, BF16)
        kr_ref[...] = jnp.zeros(kr_ref.shape, F32)

    _mla_heads(tok, cq_ref, ckv_ref, kr_ref, w_uq_ref, w_ukv_ref, qg_ref, kg_ref, cos_ref, sin_ref,
               qT_ref, k_ref, vT_ref)

    y = _rms_rows(h_ref[tok, :], g_ref[...]).astype(BF16)
    projT = lax.dot_general(w_in_ref[...], y, (((1,), (1,)), ((), ())),
                            preferred_element_type=F32)
    cq_ref[...] = _rms_cols(projT[:MLA_Q_LORA], qn_ref[...]).astype(BF16)
    ckv_ref[...] = _rms_cols(projT[MLA_Q_LORA:MLA_Q_LORA + MLA_KV_LORA], kvn_ref[...]).astype(BF16)
    kr_ref[...] = projT[MLA_Q_LORA + MLA_KV_LORA:]


def _mla_heads(tok, cq_ref, ckv_ref, kr_ref, w_uq_ref, w_ukv_ref, qg_ref, kg_ref, cos_ref, sin_ref,
               qT_ref, k_ref, vT_ref):
    tm = tok.stop - tok.start
    kr = kr_ref[...]
    qT = jnp.dot(w_uq_ref[...], cq_ref[...], preferred_element_type=F32)
    kvT = jnp.dot(w_ukv_ref[...], ckv_ref[...], preferred_element_type=F32)
    cos = cos_ref[:, tok]
    sin = sin_ref[:, tok]
    qg = qg_ref[...]
    kg = kg_ref[...]
    kr_ss = jnp.sum(kr * kr, axis=0, keepdims=True)
    scale = MLA_QK ** -0.5 * LOG2E
    half = MLA_ROPE // 2
    pad = MLA_QK_PAD - MLA_QK
    aug = MLA_V_AUG - MLA_V
    ones_tile = (lax.broadcasted_iota(jnp.int32, (aug, tm), 0) == 0).astype(F32).astype(BF16)
    for hd in range(MLA_HEADS):
        qh = _rms_cols(qT[hd * MLA_QK:(hd + 1) * MLA_QK], qg) * scale
        x1 = qh[MLA_NOPE:MLA_NOPE + half]
        x2 = qh[MLA_NOPE + half:]
        qT_ref[hd, 0:MLA_NOPE, tok] = qh[:MLA_NOPE].astype(BF16)
        qT_ref[hd, MLA_NOPE:MLA_NOPE + half, tok] = (x1 * cos - x2 * sin).astype(BF16)
        qT_ref[hd, MLA_NOPE + half:MLA_QK, tok] = (x2 * cos + x1 * sin).astype(BF16)
        qT_ref[hd, MLA_QK:, tok] = jnp.zeros((pad, tm), BF16)

        base = hd * (MLA_NOPE + MLA_V)
        kn = kvT[base:base + MLA_NOPE]
        inv = lax.rsqrt((jnp.sum(kn * kn, axis=0, keepdims=True) + kr_ss) * (1.0 / MLA_QK) + EPS)
        kn = kn * inv * kg[:MLA_NOPE]
        krh = kr * inv * kg[MLA_NOPE:]
        k1 = krh[:half]
        k2 = krh[half:]
        kT = jnp.concatenate(
            [kn, k1 * cos - k2 * sin, k2 * cos + k1 * sin, jnp.zeros((pad, tm), F32)], axis=0)
        k_ref[hd, tok, :] = kT.T.astype(BF16)
        vT_ref[hd, :MLA_V, tok] = kvT[base + MLA_NOPE:base + MLA_NOPE + MLA_V].astype(BF16)
        vT_ref[hd, MLA_V:, tok] = ones_tile


def _mla_proj(h, g, layer, w_inT, qn, kvn, w_uqT, w_ukvT, qg, kg, cosT, sinT, tm=256):
    S, D = h.shape
    tm = min(tm, S)
    H = MLA_HEADS
    n = S // tm
    latents = lambda s: jnp.minimum(s, n - 1)
    heads = lambda s: jnp.maximum(s - 1, 0)
    return pl.pallas_call(
        _mla_proj_kernel,
        grid=(n + 1,),
        in_specs=[
            pl.BlockSpec((tm, D), lambda s: (latents(s), 0)),
            _resident((1, D)),
            _resident_layer(w_inT, layer),
            _resident(qn.shape),
            _resident(kvn.shape),
            _resident_layer(w_uqT, layer),
            _resident_layer(w_ukvT, layer),
            _resident(qg.shape),
            _resident(kg.shape),
            pl.BlockSpec((MLA_ROPE // 2, tm), lambda s: (0, heads(s))),
            pl.BlockSpec((MLA_ROPE // 2, tm), lambda s: (0, heads(s))),
        ],
        out_specs=[
            pl.BlockSpec((H, MLA_QK_PAD, tm), lambda s: (0, 0, heads(s))),
            pl.BlockSpec((H, tm, MLA_QK_PAD), lambda s: (0, heads(s), 0)),
            pl.BlockSpec((H, MLA_V_AUG, tm), lambda s: (0, 0, heads(s))),
        ],
        out_shape=[
            jax.ShapeDtypeStruct((H, MLA_QK_PAD, S), BF16),
            jax.ShapeDtypeStruct((H, S, MLA_QK_PAD), BF16),
            jax.ShapeDtypeStruct((H, MLA_V_AUG, S), BF16),
        ],
        scratch_shapes=[
            pltpu.VMEM((MLA_Q_LORA, tm), BF16),
            pltpu.VMEM((MLA_KV_LORA, tm), BF16),
            pltpu.VMEM((MLA_ROPE, tm), F32),
        ],
        compiler_params=_params(("arbitrary",), 52),
        name="mla_proj",
    )(h, g, w_inT, qn, kvn, w_uqT, w_ukvT, qg, kg, cosT, sinT)


MLA_DEPTH = 4


def _mla_attn_kernel(qT_ref, k_ref, vT_ref, o_ref, s0_ref, s1_ref, s2_ref, s3_ref, p0_ref, p1_ref,
                     m_ref, alpha_ref, acc_ref, *, TK):
    i = pl.program_id(1)
    N = MLA_DEPTH
    TQ = N * TK
    s_refs = (s0_ref, s1_ref, s2_ref, s3_ref)
    p_refs = (p0_ref, p1_ref)

    def lo(u):
        return u * TK if 0 <= u < N else 0

    def scores(block, u):
        start = pl.multiple_of(block * TK, TK)
        s_refs[u % N][:, lo(u):] = jnp.dot(k_ref[0, pl.ds(start, TK), :], qT_ref[0, :, lo(u):],
                                           preferred_element_type=F32)

    def pv(block, u, trimmed=True):
        c = lo(u) if trimmed else 0
        start = pl.multiple_of(block * TK, TK)
        prod = jnp.dot(vT_ref[0, :, pl.ds(start, TK)], p_refs[u % 2][:, c:],
                       preferred_element_type=F32)
        acc_ref[:, c:] = alpha_ref[u % N, :, c:] * acc_ref[:, c:] + prod

    def running_max(u, diagonal):
        s_ref = s_refs[u % N]
        c = lo(u)
        if diagonal:
            kc = lax.broadcasted_iota(jnp.int32, (TK, TK), 0) // CHUNK
            qc = lax.broadcasted_iota(jnp.int32, (TK, TK), 1) // CHUNK
            s_ref[:, c:c + TK] = jnp.where(kc <= qc, s_ref[:, c:c + TK], -jnp.inf)
        m_prev = m_ref[(u - 1) % N, :, c:]
        m_new = jnp.maximum(m_prev, jnp.max(s_ref[:, c:], axis=0, keepdims=True))
        m_ref[u % N, :, c:] = m_new
        alpha_ref[u % N, :, c:] = jnp.exp2(m_prev - m_new)
        if c:
            m_ref[u % N, :, :c] = m_ref[(u - 1) % N, :, :c]
            alpha_ref[u % N, :, :c] = jnp.ones((1, c), F32)

    def probabilities(u, zero_hidden=False):
        c = lo(u)
        p_refs[u % 2][:, c:] = jnp.exp2(s_refs[u % N][:, c:] - m_ref[u % N, :, c:]).astype(BF16)
        if zero_hidden and c:
            p_refs[u % 2][:, :c] = jnp.zeros((TK, c), BF16)

    def step(u, block_ahead, block_behind, diagonal_next=False, last_diagonal=False):
        scores(block_ahead, u + 2)
        pv(block_behind, u - 1, trimmed=not (u - 1 == N - 1))
        probabilities(u, zero_hidden=last_diagonal)
        running_max(u + 1, diagonal_next)

    first = N * i
    acc_ref[...] = jnp.zeros(acc_ref.shape, F32)
    p_refs[1][...] = jnp.zeros(p_refs[1].shape, BF16)
    alpha_ref[N - 1] = jnp.ones(alpha_ref.shape[1:], F32)
    m_ref[N - 1] = jnp.full(m_ref.shape[1:], -jnp.inf, F32)
    scores(first, 0)
    scores(first + 1, 1)
    running_max(0, True)

    step(0, first + 2, first, diagonal_next=True)
    step(1, first + 3, first, diagonal_next=True)
    step(2, 0, first + 1, diagonal_next=True)
    step(3, 1, first + 2, last_diagonal=True)

    def four_steps(j):
        base = N * j
        step(N, base + 2, jnp.where(j == 0, first + 3, base - 1))
        step(N + 1, base + 3, base)
        step(N + 2, base + 4, base + 1)
        step(N + 3, base + 5, base + 2)

    def body(jj, carry):
        four_steps(2 * jj)
        four_steps(2 * jj + 1)
        return carry

    lax.fori_loop(0, i // 2, body, 0)

    @pl.when(i % 2 == 1)
    def _():
        four_steps(i - 1)
    pv(jnp.where(i == 0, first + 3, first - 1), N - 1, trimmed=False)
    o = acc_ref[:MLA_V, :] / acc_ref[MLA_V:MLA_V + 1, :]
    o_ref[...] = o.T.astype(BF16)


def _mla_attn(qT, k, vT, tk=512):
    H, _, S = qT.shape
    tk = min(tk, S // MLA_DEPTH)
    tq = MLA_DEPTH * tk
    return pl.pallas_call(
        functools.partial(_mla_attn_kernel, TK=tk),
        grid=(H, S // tq),
        in_specs=[
            pl.BlockSpec((1, MLA_QK_PAD, tq), lambda h, i: (h, 0, i)),
            pl.BlockSpec((1, S, MLA_QK_PAD), lambda h, i: (h, 0, 0)),
            pl.BlockSpec((1, MLA_V_AUG, S), lambda h, i: (h, 0, 0)),
        ],
        out_specs=pl.BlockSpec((tq, MLA_V), lambda h, i: (i, h)),
        out_shape=jax.ShapeDtypeStruct((S, H * MLA_V), BF16),
        scratch_shapes=(
            [pltpu.VMEM((tk, tq), F32)] * MLA_DEPTH
            + [pltpu.VMEM((tk, tq), BF16)] * 2
            + [pltpu.VMEM((MLA_DEPTH, 1, tq), F32),
               pltpu.VMEM((MLA_DEPTH, 1, tq), F32),
               pltpu.VMEM((MLA_V_AUG, tq), F32)]
        ),
        compiler_params=_params(("parallel", "arbitrary"), 54),
        name="mla_attn",
    )(qT, k, vT)


def _sb_proj_kernel(h_ref, g_ref, w_ref, q_ref, kT_ref, v_ref):
    HD = SB_HEADS * SB_HEAD_DIM
    y = _rms_rows(h_ref[...], g_ref[...]).astype(BF16)
    qkv = jnp.dot(y, w_ref[...], preferred_element_type=F32)
    q_ref[...] = (qkv[:, :HD] * (SB_HEAD_DIM ** -0.5 * LOG2E)).astype(BF16)
    v_ref[...] = qkv[:, 2 * HD:].astype(BF16)
    for hd in range(SB_HEADS):
        k = qkv[:, HD + hd * SB_HEAD_DIM:HD + (hd + 1) * SB_HEAD_DIM]
        kT_ref[hd] = k.T.astype(BF16)


def _sb_proj(h, g, layer, w, tm=256):
    S, D = h.shape
    tm = min(tm, S)
    HD = SB_HEADS * SB_HEAD_DIM
    return pl.pallas_call(
        _sb_proj_kernel,
        grid=(S // tm,),
        in_specs=[
            pl.BlockSpec((tm, D), lambda i: (i, 0)),
            _resident((1, D)),
            _resident_layer(w, layer),
        ],
        out_specs=[
            pl.BlockSpec((tm, HD), lambda i: (i, 0)),
            pl.BlockSpec((SB_HEADS, SB_HEAD_DIM, tm), lambda i: (0, 0, i)),
            pl.BlockSpec((tm, HD), lambda i: (i, 0)),
        ],
        out_shape=[
            jax.ShapeDtypeStruct((S, HD), BF16),
            jax.ShapeDtypeStruct((SB_HEADS, SB_HEAD_DIM, S), BF16),
            jax.ShapeDtypeStruct((S, HD), BF16),
        ],
        compiler_params=_params(("parallel",), 52),
        name="sb_proj",
    )(h, g, w)


def _sb_attn_kernel(q_ref, kT_ref, v_ref, o_ref, za_ref, zb_ref, aa_ref, ab_ref,
                    acc_ref, after_ref, later_ref, *, TK, R):
    i = pl.program_id(1)
    row = lax.broadcasted_iota(jnp.int32, (TK, TK), 0)
    col = lax.broadcasted_iota(jnp.int32, (TK, TK), 1)
    later_ref[...] = (row > col).astype(F32).astype(BF16)
    last = R * i + R - 1

    def key_start(t):
        return pl.multiple_of(jnp.maximum(last - t, 0) * TK, TK)

    def logits(t, z_ref):
        z_ref[...] = jnp.dot(q_ref[...], kT_ref[0, :, pl.ds(key_start(t), TK)],
                             preferred_element_type=F32)

    def pending_av(t, a_ref):
        acc_ref[...] += jnp.dot(a_ref[...], v_ref[pl.ds(key_start(jnp.maximum(t, 0)), TK), :],
                                preferred_element_type=F32)

    def weights(z_ref, a_ref, mask):
        for r in range(R):
            rows = slice(r * TK, (r + 1) * TK)
            kind = None if mask is None else mask[r]
            if kind == "none":
                a_ref[rows, :] = jnp.zeros((TK, TK), BF16)
                continue
            z = z_ref[rows, :]
            sp = jnp.maximum(z, 0.0) + jnp.log2(1.0 + jnp.exp2(_neg_abs(z)))
            if kind == "diag":
                valid = col < row
                sp = jnp.where(valid, sp, 0.0)
            within = jnp.dot(sp.astype(BF16), later_ref[...], preferred_element_type=F32)
            after = after_ref[rows, :]
            a = jnp.exp2(z - sp - within - after)
            if kind == "diag":
                a = jnp.where(valid, a, 0.0)
            a_ref[rows, :] = a.astype(BF16)
            after_ref[rows, :] = after + jnp.sum(sp, axis=1, keepdims=True)

    def pair(t, masks):
        logits(t + 1, zb_ref)
        pending_av(t - 1, ab_ref)
        weights(za_ref, aa_ref, masks[0])
        logits(t + 2, za_ref)
        pending_av(t, aa_ref)
        weights(zb_ref, ab_ref, masks[1])

    acc_ref[...] = jnp.zeros(acc_ref.shape, F32)
    after_ref[...] = jnp.zeros(after_ref.shape, F32)
    ab_ref[...] = jnp.zeros(ab_ref.shape, BF16)
    logits(0, za_ref)

    def diagonal_mask(t):
        beside = R - 1 - t
        return tuple("diag" if r == beside else ("none" if r < beside else None) for r in range(R))

    for t in range(0, R, 2):
        pair(t, (diagonal_mask(t), diagonal_mask(t + 1)))

    def body(j, carry):
        for t in range(0, R, 2):
            pair(R * j + R + t, (None, None))
        return carry

    lax.fori_loop(0, i, body, 0)
    pending_av(last, ab_ref)
    o_ref[...] = acc_ref[...].astype(BF16)


def _sb_attn(q, kT, v, tk=256, r=4):
    S = q.shape[0]
    tk = min(tk, S // r)
    tq = r * tk
    Dh = SB_HEAD_DIM
    return pl.pallas_call(
        functools.partial(_sb_attn_kernel, TK=tk, R=r),
        grid=(SB_HEADS, S // tq),
        in_specs=[
            pl.BlockSpec((tq, Dh), lambda h, i: (i, h)),
            pl.BlockSpec((1, Dh, S), lambda h, i: (h, 0, 0)),
            pl.BlockSpec((S, Dh), lambda h, i: (0, h)),
        ],
        out_specs=pl.BlockSpec((tq, Dh), lambda h, i: (i, h)),
        out_shape=jax.ShapeDtypeStruct((S, SB_HEADS * Dh), BF16),
        scratch_shapes=[
            pltpu.VMEM((tq, tk), F32),
            pltpu.VMEM((tq, tk), F32),
            pltpu.VMEM((tq, tk), BF16),
            pltpu.VMEM((tq, tk), BF16),
            pltpu.VMEM((tq, Dh), F32),
            pltpu.VMEM((tq, 1), F32),
            pltpu.VMEM((tk, tk), BF16),
        ],
        compiler_params=_params(("parallel", "arbitrary"), 48),
        name="sb_attn",
    )(q, kT, v)


def _out_proj_kernel(o_ref, h_ref, w_ref, g_ref, h1_ref, xn_ref):
    h1 = h_ref[...] + jnp.dot(o_ref[...], w_ref[...], preferred_element_type=F32)
    h1_ref[...] = h1
    xn_ref[...] = _rms_rows(h1, g_ref[...]).astype(BF16)


def _out_proj(o, h, layer, w, g, tm=256):
    S, D = h.shape
    tm = min(tm, S)
    return pl.pallas_call(
        _out_proj_kernel,
        grid=(S // tm,),
        in_specs=[
            pl.BlockSpec((tm, o.shape[1]), lambda i: (i, 0)),
            pl.BlockSpec((tm, D), lambda i: (i, 0)),
            _resident_layer(w, layer),
            _resident((1, D)),
        ],
        out_specs=[
            pl.BlockSpec((tm, D), lambda i: (i, 0)),
            pl.BlockSpec((tm, D), lambda i: (i, 0)),
        ],
        out_shape=[
            jax.ShapeDtypeStruct((S, D), F32),
            jax.ShapeDtypeStruct((S, D), BF16),
        ],
        compiler_params=_params(("parallel",), 48),
        name="out_proj",
    )(o, h, w, g)


def _mlp_up_kernel(x_ref, w_ref, wd_ref, wg_ref, u_ref, wd_bf16_ref, wg_bf16_ref):
    u = jnp.dot(x_ref[...], w_ref[...].astype(BF16), preferred_element_type=F32)
    u = jnp.maximum(u, 0.0)
    u_ref[...] = (u * u).astype(BF16)
    wd_bf16_ref[...] = wd_ref[...].astype(BF16)
    wg_bf16_ref[...] = wg_ref[...].astype(BF16)


def _mlp_up(xn, layer, w, w_down, w_gate, tm=2048, tn=512):
    S, D = xn.shape
    F = w.shape[2]
    tm, tn = min(tm, S), min(tn, F)
    ni, nj = S // tm, F // tn
    steps = ni * nj
    rows_d, rows_g = F // steps, D // steps
    assert rows_d * steps == F and rows_g * steps == D and rows_g % 16 == 0
    slab = lambda i, j: i * nj + j
    return pl.pallas_call(
        _mlp_up_kernel,
        grid=(ni, nj),
        in_specs=[
            pl.BlockSpec((tm, D), lambda i, j: (i, 0), pipeline_mode=pl.Buffered(1)),
            pl.BlockSpec((None, D, tn), lambda i, j: (layer, 0, j)),
            pl.BlockSpec((None, rows_d, D), lambda i, j: (layer, slab(i, j), 0)),
            pl.BlockSpec((None, rows_g, D), lambda i, j: (layer, slab(i, j), 0)),
        ],
        out_specs=[
            pl.BlockSpec((tm, tn), lambda i, j: (i, j)),
            pl.BlockSpec((rows_d, D), lambda i, j: (slab(i, j), 0)),
            pl.BlockSpec((rows_g, D), lambda i, j: (slab(i, j), 0)),
        ],
        out_shape=[
            jax.ShapeDtypeStruct((S, F), BF16),
            jax.ShapeDtypeStruct((F, D), BF16),
            jax.ShapeDtypeStruct((D, D), BF16),
        ],
        compiler_params=_params(("parallel", "arbitrary"), 52),
        name="mlp_up",
    )(xn, w, w_down, w_gate)


def _mlp_down_kernel(u_ref, w_ref, res_hbm, g_ref, o_ref, hg_ref, ss_ref, res_buf, res_sem):
    i, j, k = pl.program_id(0), pl.program_id(1), pl.program_id(2)
    tm, tn = res_buf.shape

    def residual_copy():
        rows = pl.ds(pl.multiple_of(i * tm, tm), tm)
        cols = pl.ds(pl.multiple_of(j * tn, tn), tn)
        return pltpu.make_async_copy(res_hbm.at[rows, cols], res_buf, res_sem)

    @pl.when(k == 0)
    def _():
        residual_copy().start()
        o_ref[...] = jnp.dot(u_ref[...], w_ref[...], preferred_element_type=F32)

    @pl.when(k > 0)
    def _():
        o_ref[...] += jnp.dot(u_ref[...], w_ref[...], preferred_element_type=F32)

    @pl.when(k == pl.num_programs(2) - 1)
    def _():
        residual_copy().wait()
        o = o_ref[...] + res_buf[...]
        o_ref[...] = o
        hg_ref[...] = (o * g_ref[...]).astype(BF16)
        ss_ref[...] = jnp.broadcast_to(jnp.sum(o * o, axis=-1, keepdims=True), ss_ref.shape)


def _mlp_down(u, w, res, g, tm=1024, tn=1024, tk=4096):
    S, F = u.shape
    D = w.shape[1]
    tm, tn, tk = min(tm, S), min(tn, D), min(tk, F)
    return pl.pallas_call(
        _mlp_down_kernel,
        grid=(S // tm, D // tn, F // tk),
        in_specs=[
            pl.BlockSpec((tm, tk), lambda i, j, k: (i, k)),
            pl.BlockSpec((tk, tn), lambda i, j, k: (k, j)),
            pl.BlockSpec(memory_space=pl.ANY),
            pl.BlockSpec((1, tn), lambda i, j, k: (0, j)),
        ],
        out_specs=[
            pl.BlockSpec((tm, tn), lambda i, j, k: (i, j)),
            pl.BlockSpec((tm, tn), lambda i, j, k: (i, j)),
            pl.BlockSpec((tm, LANES), lambda i, j, k: (i, j)),
        ],
        out_shape=[
            jax.ShapeDtypeStruct((S, D), F32),
            jax.ShapeDtypeStruct((S, D), BF16),
            jax.ShapeDtypeStruct((S, (D // tn) * LANES), F32),
        ],
        scratch_shapes=[pltpu.VMEM((tm, tn), F32), pltpu.SemaphoreType.DMA(())],
        compiler_params=_params(("arbitrary", "arbitrary", "arbitrary"), 58),
        name="mlp_down",
    )(u, w, res, g)


PLE_ROW_GROUPS = 4


def _ple_kernel(hg_ref, ss_ref, h_ref, wg_ref, p_ref, wp_ref, o_ref, *, d_model):
    rows_per_group = hg_ref.shape[0] // PLE_ROW_GROUPS
    for r in range(PLE_ROW_GROUPS):
        rows = slice(r * rows_per_group, (r + 1) * rows_per_group)
        ss = ss_ref[rows, :]
        total = ss[:, 0:1]
        for t in range(1, ss.shape[1] // LANES):
            total = total + ss[:, t * LANES:t * LANES + 1]
        inv = lax.rsqrt(total * (1.0 / d_model) + EPS)
        logits = jnp.dot(hg_ref[rows, :], wg_ref[...], preferred_element_type=F32) * inv
        gate = 1.0 / (1.0 + jnp.exp(-logits))
        emb = jnp.dot(p_ref[rows, :].astype(BF16), wp_ref[...], preferred_element_type=F32)
        o_ref[rows, :] = h_ref[rows, :] + emb * gate


def _ple(h, hg, ss, layer, wg, p, wp, tm=1024, tn=512):
    S, D = h.shape
    P = p.shape[-1]
    tm, tn = min(tm, S), min(tn, D)
    return pl.pallas_call(
        functools.partial(_ple_kernel, d_model=D),
        grid=(S // tm, D // tn),
        in_specs=[
            pl.BlockSpec((tm, D), lambda i, j: (i, 0)),
            pl.BlockSpec((tm, ss.shape[1]), lambda i, j: (i, 0)),
            pl.BlockSpec((tm, tn), lambda i, j: (i, j)),
            pl.BlockSpec((D, tn), lambda i, j: (0, j)),
            pl.BlockSpec((None, None, tm, P), lambda i, j: (layer, 0, i, 0)),
            pl.BlockSpec((None, P, tn), lambda i, j: (layer, 0, j)),
        ],
        out_specs=pl.BlockSpec((tm, tn), lambda i, j: (i, j)),
        out_shape=jax.ShapeDtypeStruct((S, D), F32),
        compiler_params=_params(("parallel", "arbitrary"), 48),
        name="ple_gate",
    )(hg, ss, h, wg, p, wp)


def _rope_tables(positions):
    inv_freq = ROPE_THETA ** (-jnp.arange(0, MLA_ROPE, 2, dtype=F32) / MLA_ROPE)
    ang = positions.astype(F32)[:, None] * inv_freq
    return jnp.cos(ang).T, jnp.sin(ang).T


def kernel(x, p, positions, norm_mix, norm_mlp, norm_ple, mla_w_in, mla_q_norm, mla_kv_norm, mla_w_uq, mla_w_ukv, mla_q_gain, mla_k_gain, mla_w_o, sb_w_qkv, sb_w_o, mlp_w_up, mlp_w_down, ple_w_proj, ple_w_gate):
    assert x.shape[0] == 1, "one sequence per call"
    depth = p.shape[0]
    cosT, sinT = _rope_tables(positions[0])
    col = lambda v: v.astype(F32)[:, None]
    row = lambda v: v.astype(F32)[None, :]
    transposed = lambda w: jnp.swapaxes(w, 1, 2).astype(BF16)
    mla_in_t, mla_uq_t, mla_ukv_t = transposed(mla_w_in), transposed(mla_w_uq), transposed(mla_w_ukv)
    mla_o, sb_qkv, sb_o = mla_w_o.astype(BF16), sb_w_qkv.astype(BF16), sb_w_o.astype(BF16)
    w_proj = ple_w_proj.astype(BF16)
    h = x[0]
    for i in range(depth):
        j = i // N_MIXERS
        if i % N_MIXERS == 0:
            qT, k, vT = _mla_proj(
                h, row(norm_mix[i]), j, mla_in_t, col(mla_q_norm[j]), col(mla_kv_norm[j]),
                mla_uq_t, mla_ukv_t, col(mla_q_gain[j]), col(mla_k_gain[j]), cosT, sinT)
            o = _mla_attn(qT, k, vT)
            w_o = mla_o
        else:
            q, kT, v = _sb_proj(h, row(norm_mix[i]), j, sb_qkv)
            o = _sb_attn(q, kT, v)
            w_o = sb_o
        h1, xn = _out_proj(o, h, j, w_o, row(norm_mlp[i]))
        u, w_down, w_gate = _mlp_up(xn, i, mlp_w_up, mlp_w_down, ple_w_gate)
        h2, hg, ss = _mlp_down(u, w_down, h1, row(norm_ple[i]))
        h = _ple(h2, hg, ss, i, w_gate, p, w_proj)
    return h[None]
```
